```python
import jax, jax.numpy as jnp
from jax import lax
import numpy as np

D_MODEL = 2048
BATCH = 2
SEQ = 8192
DEPTH = 1

HEAD_DIM = 64
W_A = D_MODEL // 2
W_B = D_MODEL - W_A
N_HEADS_A = W_A // HEAD_DIM
N_HEADS_B = W_B // HEAD_DIM
CONV_A = 3
CONV_B = 31
PROJ_W = 3 * W_A + 2 * W_B
N_EXPERTS = 32
TOP_K = 4
D_FF = D_MODEL
SWIGLU_ALPHA = 1.702
SWIGLU_LIMIT = 7.0
CHUNK = 256
EPS = 1e-6
N_MOD = 6

kernel_name = "hybrid_conv_groups_moe_block"


def rms_norm(x, g):
    xf = x.astype(jnp.float32)
    y = xf * lax.rsqrt(jnp.mean(xf * xf, axis=-1, keepdims=True) + EPS)
    return (y * g.astype(jnp.float32)).astype(x.dtype)


def layer_norm(x, g, b):
    xf = x.astype(jnp.float32)
    mu = jnp.mean(xf, axis=-1, keepdims=True)
    var = jnp.mean(jnp.square(xf - mu), axis=-1, keepdims=True)
    y = (xf - mu) * lax.rsqrt(var + EPS)
    return (y * g.astype(jnp.float32) + b.astype(jnp.float32)).astype(x.dtype)


def modulate(h, shift, scale):
    return h * (1 + scale[:, None, :]) + shift[:, None, :]


def causal_depthwise_conv(x, w):
    k, ch = w.shape
    return lax.conv_general_dilated(
        x, w[:, None, :].astype(x.dtype), window_strides=(1,), padding=((k - 1, 0),),
        dimension_numbers=("NWC", "WIO", "NWC"), feature_group_count=ch)


def hybrid_mixer(h, w_in, conv_a_w, conv_b_w, conv_b_b, ln_b_g, ln_b_b, w_out):
    proj = h @ w_in
    a_pre, a_post, a_val, b_val, b_gate = jnp.split(
        proj, [W_A, 2 * W_A, 3 * W_A, 3 * W_A + W_B], axis=-1)
    y_a = a_post * causal_depthwise_conv(a_pre * a_val, conv_a_w)
    u = b_val * jax.nn.sigmoid(b_gate)
    u = causal_depthwise_conv(u, conv_b_w) + conv_b_b
    y_b = jax.nn.silu(layer_norm(u, ln_b_g, ln_b_b))
    return jnp.concatenate([y_a, y_b], axis=-1) @ w_out


def clamped_swiglu(gate, up):
    gate = jnp.minimum(gate, SWIGLU_LIMIT)
    up = jnp.clip(up, -SWIGLU_LIMIT, SWIGLU_LIMIT)
    return (up + 1) * (gate * jax.nn.sigmoid(SWIGLU_ALPHA * gate))


def moe_ffn(h, w_router, b_router, w_gate, b_gate, w_up, b_up, w_down, b_down):
    bsz, t, d = h.shape
    n = bsz * t
    hf = h.reshape(n, d)
    logits = (hf @ w_router + b_router).astype(jnp.float32)
    top_vals, top_idx = lax.top_k(logits, TOP_K)
    top_w = jax.nn.softmax(top_vals, axis=-1).astype(h.dtype)
    nk = n * TOP_K
    flat_e = top_idx.reshape(nk).astype(jnp.int32)
    flat_tok = jnp.arange(nk, dtype=jnp.int32) // TOP_K
    order = jnp.argsort(flat_e)
    sorted_e = flat_e[order]
    sorted_tok = flat_tok[order]
    counts = jnp.bincount(flat_e, length=N_EXPERTS).astype(jnp.int32)
    padded = ((counts + CHUNK - 1) // CHUNK) * CHUNK
    group_start = jnp.cumsum(counts) - counts
    pad_end = jnp.cumsum(padded)
    pad_start = pad_end - padded
    rank = jnp.arange(nk, dtype=jnp.int32) - group_start[sorted_e]
    dest = (pad_start[sorted_e] + rank).astype(jnp.int32)
    n_chunks = -(-nk // CHUNK) + N_EXPERTS
    p = n_chunks * CHUNK
    tok_buf = jnp.full((p,), n, jnp.int32).at[dest].set(sorted_tok)
    chunk_e = jnp.minimum(
        jnp.searchsorted(pad_end, jnp.arange(n_chunks, dtype=jnp.int32) * CHUNK, side="right"),
        N_EXPERTS - 1).astype(jnp.int32)
    hpad = jnp.concatenate([hf, jnp.zeros((1, d), hf.dtype)], axis=0)

    def expert_block(args):
        ids, e = args
        xc = hpad[ids]
        g = xc @ w_gate[e] + b_gate[e]
        u = xc @ w_up[e] + b_up[e]
        return clamped_swiglu(g, u) @ w_down[e] + b_down[e]

    out_buf = lax.map(expert_block, (tok_buf.reshape(n_chunks, CHUNK), chunk_e)).reshape(p, d)
    dest_orig = jnp.zeros((nk,), jnp.int32).at[order].set(dest)
    y = jnp.einsum("nkd,nk->nd", out_buf[dest_orig].reshape(n, TOP_K, d), top_w)
    return y.reshape(bsz, t, d)


def setup_inputs(seed: int = 0) -> dict:
    key = jax.random.key(seed)
    ks = jax.random.split(key, 24)
    L, D, E, F = DEPTH, D_MODEL, N_EXPERTS, D_FF
    nrm = lambda k, shape, s: (jax.random.normal(k, shape, jnp.float32) * s)
    gain = lambda k, shape: 1.0 + 0.05 * jax.random.normal(k, shape, jnp.float32)
    return {
        "x": jax.random.normal(ks[0], (BATCH, SEQ, D), jnp.float32),
        "c": jax.random.normal(ks[1], (BATCH, D), jnp.float32),
        "w_ada": nrm(ks[2], (L, D, N_MOD * D), 0.5 * D ** -0.5),
        "b_ada": nrm(ks[3], (L, N_MOD * D), 0.02),
        "g_pre_mix": gain(ks[4], (L, D)),
        "g_post_mix": gain(ks[5], (L, D)),
        "w_in": nrm(ks[6], (L, D, PROJ_W), D ** -0.5),
        "conv_a_w": nrm(ks[7], (L, CONV_A, W_A), CONV_A ** -0.5),
        "conv_b_w": nrm(ks[8], (L, CONV_B, W_B), CONV_B ** -0.5),
        "conv_b_b": nrm(ks[9], (L, W_B), 0.02),
        "ln_b_g": gain(ks[10], (L, W_B)),
        "ln_b_b": nrm(ks[11], (L, W_B), 0.02),
        "w_out": nrm(ks[12], (L, D, D), D ** -0.5),
        "g_pre_ffn": gain(ks[13], (L, D)),
        "g_post_ffn": gain(ks[14], (L, D)),
        "w_router": nrm(ks[15], (L, D, E), D ** -0.5),
        "b_router": nrm(ks[16], (L, E), 0.01),
        "w_gate": nrm(ks[17], (L, E, D, F), D ** -0.5),
        "b_gate": nrm(ks[18], (L, E, F), 0.02),
        "w_up": nrm(ks[19], (L, E, D, F), D ** -0.5),
        "b_up": nrm(ks[20], (L, E, F), 0.02),
        "w_down": nrm(ks[21], (L, E, F, D), F ** -0.5),
        "b_down": nrm(ks[22], (L, E, D), 0.02),
    }


def reference(x, c, w_ada, b_ada, g_pre_mix, g_post_mix, w_in, conv_a_w, conv_b_w, conv_b_b,
              ln_b_g, ln_b_b, w_out, g_pre_ffn, g_post_ffn, w_router, b_router,
              w_gate, b_gate, w_up, b_up, w_down, b_down):
    c_act = jax.nn.silu(c)
    for l in range(DEPTH):
        mod = c_act @ w_ada[l] + b_ada[l]
        sh_m, sc_m, gt_m, sh_f, sc_f, gt_f = jnp.split(mod, N_MOD, axis=-1)
        h = modulate(rms_norm(x, g_pre_mix[l]), sh_m, sc_m)
        y = hybrid_mixer(h, w_in[l], conv_a_w[l], conv_b_w[l], conv_b_b[l],
                         ln_b_g[l], ln_b_b[l], w_out[l])
        x = x + gt_m[:, None, :] * rms_norm(y, g_post_mix[l])
        h = modulate(rms_norm(x, g_pre_ffn[l]), sh_f, sc_f)
        y = moe_ffn(h, w_router[l], b_router[l], w_gate[l], b_gate[l],
                    w_up[l], b_up[l], w_down[l], b_down[l])
        x = x + gt_f[:, None, :] * rms_norm(y, g_post_ffn[l])
    return x
```

```python
import functools

import jax
import jax.numpy as jnp
from jax import lax
from jax.experimental import pallas as pl
from jax.experimental.pallas import tpu as pltpu

EPS = 1e-6
TOP_K = 4
CONV_A = 3
CONV_B = 31
SWIGLU_ALPHA = 1.702
SWIGLU_LIMIT = 7.0

LANES = 128
SUBLANES = 8
VMEM_LIMIT_BYTES = 56 * 1024 * 1024

ADA_TN = 1024
MIX_TM = 256
HALO_A = 8
HALO_B = 32
PLAN_TM = 2048
DISPATCH_TM = 1024
SUB = 256
EXPERT_TM = 1024
EXPERT_TF = 256
COMBINE_TM = 256
NEG_BIG = -1e30


def _cparams(sem):
    return pltpu.CompilerParams(dimension_semantics=sem, vmem_limit_bytes=VMEM_LIMIT_BYTES)


def _rms(x, g):
    return x * lax.rsqrt(jnp.mean(x * x, axis=-1, keepdims=True) + EPS) * g


def _pack_slab_words(v):
    half = v.shape[1] // 2
    lo = pltpu.bitcast(v[:, :half].astype(jnp.bfloat16).astype(jnp.float32), jnp.uint32)
    hi = pltpu.bitcast(v[:, half:].astype(jnp.bfloat16).astype(jnp.float32), jnp.uint32)
    return (lo >> 16) | (hi & jnp.uint32(0xFFFF0000))


def _unpack_slab_words(w):
    lo = pltpu.bitcast(w << 16, jnp.float32)
    hi = pltpu.bitcast(w & jnp.uint32(0xFFFF0000), jnp.float32)
    return lo, hi


def _ada_kernel(c_ref, w_ref, b_ref, o_ref):
    c = c_ref[...]
    ca = (c * jax.nn.sigmoid(c)).astype(jnp.bfloat16)
    o_ref[...] = jnp.dot(ca, w_ref[...].astype(jnp.bfloat16),
                         preferred_element_type=jnp.float32) + b_ref[...]


def _ada(c, w_ada, b_ada):
    bsz, d = c.shape
    n_out = w_ada.shape[1]
    tn = min(ADA_TN, n_out)
    assert n_out % tn == 0 and bsz <= SUBLANES
    c8 = jnp.zeros((SUBLANES, d), jnp.float32).at[:bsz].set(c)
    out = pl.pallas_call(
        _ada_kernel,
        grid=(n_out // tn,),
        in_specs=[pl.BlockSpec((SUBLANES, d), lambda j: (0, 0)),
                  pl.BlockSpec((d, tn), lambda j: (0, j)),
                  pl.BlockSpec((1, tn), lambda j: (0, j))],
        out_specs=pl.BlockSpec((SUBLANES, tn), lambda j: (0, j)),
        out_shape=jax.ShapeDtypeStruct((SUBLANES, n_out), jnp.float32),
        compiler_params=_cparams(("arbitrary",)),
        name="ada",
    )(c8, w_ada, b_ada.reshape(1, n_out))
    return out[:bsz]


def _mix_kernel(x_ref, mod_ref, gpre_ref, gpost_ref, gffn_ref, win_ref, caw_ref, cbw_ref,
                cbb_ref, lng_ref, lnb_ref, wout_ref, wrh_ref, wrl_ref, br_ref,
                x1_ref, hp_ref, idx_ref, wgt_ref, rank_ref, cnt_ref,
                pa_buf, u_buf, v_buf, ycat, run_cnt, *, tiles_per_seq, wa, wb):
    i = pl.program_id(0)
    tm, d = x_ref.shape
    f32, bf16 = jnp.float32, jnp.bfloat16

    @pl.when(i == 0)
    def _():
        run_cnt[...] = jnp.zeros_like(run_cnt)

    @pl.when(i % tiles_per_seq == 0)
    def _():
        pa_buf[:, 0:HALO_A, :] = jnp.zeros((wa // LANES, HALO_A, LANES), f32)
        u_buf[:, 0:HALO_B, :] = jnp.zeros((wb // LANES, HALO_B, LANES), f32)

    x = x_ref[...]
    mod = mod_ref[0]
    h = (_rms(x, gpre_ref[...]) * (1.0 + mod[1:2]) + mod[0:1]).astype(bf16)

    proj_a = jnp.dot(h, win_ref[:, 0:3 * wa], preferred_element_type=f32)
    for cb in range(wa // LANES):
        sl = slice(cb * LANES, (cb + 1) * LANES)
        pa = proj_a[:, sl] * proj_a[:, 2 * wa + cb * LANES:2 * wa + (cb + 1) * LANES]
        pa_buf[cb, HALO_A:HALO_A + tm, :] = pa
        conv = caw_ref[CONV_A - 1:CONV_A, sl] * pa
        for k in range(CONV_A - 1):
            off = HALO_A - (CONV_A - 1) + k
            conv = conv + caw_ref[k:k + 1, sl] * pa_buf[cb, off:off + tm, :]
        ycat[:, sl] = (proj_a[:, wa + cb * LANES:wa + (cb + 1) * LANES] * conv).astype(bf16)
        pa_buf[cb, 0:HALO_A, :] = pa_buf[cb, tm:tm + HALO_A, :]

    proj_b = jnp.dot(h, win_ref[:, 3 * wa:3 * wa + 2 * wb], preferred_element_type=f32)
    for cb in range(wb // LANES):
        sl = slice(cb * LANES, (cb + 1) * LANES)
        u = proj_b[:, sl] * jax.nn.sigmoid(proj_b[:, wb + cb * LANES:wb + (cb + 1) * LANES])
        u_buf[cb, HALO_B:HALO_B + tm, :] = u
        acc = cbb_ref[:, sl] + cbw_ref[CONV_B - 1:CONV_B, sl] * u
        for k in range(CONV_B - 1):
            off = HALO_B - (CONV_B - 1) + k
            acc = acc + cbw_ref[k:k + 1, sl] * u_buf[cb, off:off + tm, :]
        v_buf[:, sl] = acc
        u_buf[cb, 0:HALO_B, :] = u_buf[cb, tm:tm + HALO_B, :]
    v = v_buf[...]
    mu = jnp.mean(v, axis=-1, keepdims=True)
    vc = v - mu
    var = jnp.mean(vc * vc, axis=-1, keepdims=True)
    yb = vc * lax.rsqrt(var + EPS) * lng_ref[...] + lnb_ref[...]
    ycat[:, wa:wa + wb] = (yb * jax.nn.sigmoid(yb)).astype(bf16)

    y = jnp.dot(ycat[...], wout_ref[...], preferred_element_type=f32)
    x1 = x + mod[2:3] * _rms(y, gpost_ref[...])
    x1_ref[...] = x1

    h2 = _rms(x1, gffn_ref[...]) * (1.0 + mod[4:5]) + mod[3:4]
    words = _pack_slab_words(h2)
    slab_rows = d // (2 * LANES)
    for j in range(slab_rows):
        hp_ref[pl.ds(j, tm, stride=slab_rows), :] = words[:, j * LANES:(j + 1) * LANES]

    h2_hi = h2.astype(bf16)
    h2_lo = (h2 - h2_hi.astype(f32)).astype(bf16)
    logits = (jnp.dot(h2_hi, wrh_ref[...], preferred_element_type=f32)
              + jnp.dot(h2_hi, wrl_ref[...], preferred_element_type=f32)
              + jnp.dot(h2_lo, wrh_ref[...], preferred_element_type=f32)
              + br_ref[...])

    lane = lax.broadcasted_iota(jnp.int32, (tm, LANES), 1)
    lane_f = lane.astype(f32)
    vals, idxs = [], []
    cur = logits
    for _ in range(TOP_K):
        m = jnp.max(cur, axis=-1, keepdims=True)
        ix = jnp.min(jnp.where(cur == m, lane_f, float(LANES)), axis=-1,
                     keepdims=True).astype(jnp.int32)
        vals.append(m)
        idxs.append(ix)
        cur = jnp.where(lane == ix, -jnp.inf, cur)
    exps = [jnp.exp(vk - vals[0]) for vk in vals]
    denom = exps[0]
    for ek in exps[1:]:
        denom = denom + ek
    inv = 1.0 / denom

    row = lax.broadcasted_iota(jnp.int32, (tm, tm), 0)
    col = lax.broadcasted_iota(jnp.int32, (tm, tm), 1)
    tri = jnp.where(col < row, 1.0, 0.0).astype(bf16)
    run = run_cnt[...]
    idx_out = jnp.zeros((tm, LANES), jnp.int32)
    wgt_out = jnp.zeros((tm, LANES), f32)
    rank_out = jnp.zeros((tm, LANES), f32)
    for k in range(TOP_K):
        oh = jnp.where(lane == idxs[k], 1.0, 0.0)
        before = jnp.dot(tri, oh.astype(bf16), preferred_element_type=f32)
        rank_k = jnp.sum(oh * (before + run), axis=-1, keepdims=True)
        run = run + jnp.sum(oh, axis=0, keepdims=True)
        idx_out = jnp.where(lane == k, idxs[k], idx_out)
        wgt_out = jnp.where(lane == k, exps[k] * inv, wgt_out)
        rank_out = jnp.where(lane == k, rank_k, rank_out)
    run_cnt[...] = run
    idx_ref[...] = idx_out
    wgt_ref[...] = wgt_out
    rank_ref[...] = rank_out.astype(jnp.int32)
    cnt_ref[...] = jnp.broadcast_to(run, cnt_ref.shape).astype(jnp.int32)


def _mix(x2, mod3, seq, g_pre, g_post, g_ffn, w_in, conv_a_w, conv_b_w, conv_b_b, ln_g, ln_b,
         w_out, wr_hi, wr_lo, br):
    n, d = x2.shape
    wa = conv_a_w.shape[1]
    wb = conv_b_w.shape[1]
    tm = MIX_TM
    assert seq % tm == 0 and tm >= HALO_B and wa % LANES == 0 and wb % LANES == 0
    assert wa + wb == d and d % (2 * LANES) == 0
    tiles_per_seq = seq // tm
    slab_rows = d // (2 * LANES)
    const = lambda shape: pl.BlockSpec(shape, lambda i: (0,) * len(shape))
    resident = lambda shape: pl.BlockSpec(shape, lambda i: (0,) * len(shape),
                                          pipeline_mode=pl.Buffered(1))
    row_block = lambda w: pl.BlockSpec((tm, w), lambda i: (i, 0))
    kernel = functools.partial(_mix_kernel, tiles_per_seq=tiles_per_seq, wa=wa, wb=wb)
    return pl.pallas_call(
        kernel,
        grid=(n // tm,),
        in_specs=[row_block(d),
                  pl.BlockSpec((1, mod3.shape[1], d), lambda i: (i // tiles_per_seq, 0, 0)),
                  const((1, d)), const((1, d)), const((1, d)),
                  resident(w_in.shape),
                  const(conv_a_w.shape), const(conv_b_w.shape),
                  const((1, wb)), const((1, wb)), const((1, wb)),
                  resident(w_out.shape),
                  const(wr_hi.shape), const(wr_lo.shape), const((1, LANES))],
        out_specs=[row_block(d),
                   pl.BlockSpec((tm * slab_rows, LANES), lambda i: (i, 0)),
                   row_block(LANES), row_block(LANES), row_block(LANES),
                   const((SUBLANES, LANES))],
        out_shape=[jax.ShapeDtypeStruct((n, d), jnp.float32),
                   jax.ShapeDtypeStruct((n * slab_rows, LANES), jnp.uint32),
                   jax.ShapeDtypeStruct((n, LANES), jnp.int32),
                   jax.ShapeDtypeStruct((n, LANES), jnp.float32),
                   jax.ShapeDtypeStruct((n, LANES), jnp.int32),
                   jax.ShapeDtypeStruct((SUBLANES, LANES), jnp.int32)],
        scratch_shapes=[pltpu.VMEM((wa // LANES, tm + HALO_A, LANES), jnp.float32),
                        pltpu.VMEM((wb // LANES, tm + HALO_B, LANES), jnp.float32),
                        pltpu.VMEM((tm, wb), jnp.float32),
                        pltpu.VMEM((tm, d), jnp.bfloat16),
                        pltpu.VMEM((1, LANES), jnp.float32)],
        compiler_params=_cparams(("arbitrary",)),
        name="mix",
    )(x2, mod3, g_pre, g_post, g_ffn, w_in, conv_a_w, conv_b_w, conv_b_b, ln_g, ln_b,
      w_out, wr_hi, wr_lo, br)


def _plan_kernel(idx_ref, rank_ref, start_ref, dest_ref):
    tm = idx_ref.shape[0]
    lane = lax.broadcasted_iota(jnp.int32, (tm, LANES), 1)
    idx = idx_ref[...]
    rank = rank_ref[...]
    start = start_ref[...]
    out = jnp.zeros((tm, LANES), jnp.int32)
    for k in range(TOP_K):
        base = jnp.sum(jnp.where(lane == idx[:, k:k + 1], start, 0.0), axis=-1, keepdims=True)
        out = jnp.where(lane == k, base.astype(jnp.int32) + rank[:, k:k + 1], out)
    dest_ref[...] = out


def _plan(idx, rank, start_row):
    n = idx.shape[0]
    tm = min(PLAN_TM, n)
    assert n % tm == 0
    blk = pl.BlockSpec((tm, LANES), lambda i: (i, 0))
    return pl.pallas_call(
        _plan_kernel,
        grid=(n // tm,),
        in_specs=[blk, blk, pl.BlockSpec((1, LANES), lambda i: (0, 0))],
        out_specs=blk,
        out_shape=jax.ShapeDtypeStruct((n, LANES), jnp.int32),
        compiler_params=_cparams(("arbitrary",)),
        name="plan",
    )(idx, rank, start_row)


def _slab(ref, row, slab_rows, count=1):
    return ref.at[pl.ds(pl.multiple_of(row * slab_rows, slab_rows), count * slab_rows)]


def _dispatch_kernel(zstart_ref, tail_ref, dest_ref, hp_ref, xs_ref, zbuf, sem, zsem,
                     *, tm, n_experts, slab_rows):
    i = pl.program_id(0)

    def zero_copy(start):
        return pltpu.make_async_copy(zbuf, _slab(xs_ref, start, slab_rows, SUB), zsem)

    @pl.when(i == 0)
    def _():
        zbuf[...] = jnp.zeros_like(zbuf)
        for parity in range(2):
            for e in range(parity, n_experts, 2):
                zero_copy(zstart_ref[e]).start()
            for e in range(parity, n_experts, 2):
                zero_copy(zstart_ref[e]).wait()

        def tail_start(t, carry):
            zero_copy(tail_ref[0] + t * SUB).start()
            return carry

        def tail_wait(t, carry):
            zero_copy(tail_ref[0] + t * SUB).wait()
            return carry
        lax.fori_loop(0, tail_ref[1], tail_start, 0)
        lax.fori_loop(0, tail_ref[1], tail_wait, 0)

    def body(t, carry):
        src = _slab(hp_ref, i * tm + t, slab_rows)
        for k in range(TOP_K):
            dst = _slab(xs_ref, dest_ref[0, 0, t * TOP_K + k], slab_rows)
            pltpu.make_async_copy(src, dst, sem).start()
        return carry
    lax.fori_loop(0, tm, body, 0)
    pltpu.make_async_copy(_slab(hp_ref, 0, slab_rows, tm * TOP_K),
                          _slab(xs_ref, 0, slab_rows, tm * TOP_K), sem).wait()


def _dispatch(hp, dest_flat, zstart, tail, p_rows, slab_rows):
    n = hp.shape[0] // slab_rows
    tm = min(DISPATCH_TM, n)
    assert n % tm == 0
    n_experts = zstart.shape[0]
    dest3 = dest_flat.reshape(n // tm, 1, tm * TOP_K)
    kernel = functools.partial(_dispatch_kernel, tm=tm, n_experts=n_experts, slab_rows=slab_rows)
    return pl.pallas_call(
        kernel,
        grid_spec=pltpu.PrefetchScalarGridSpec(
            num_scalar_prefetch=2,
            grid=(n // tm,),
            in_specs=[pl.BlockSpec((1, 1, tm * TOP_K), lambda i, *_: (i, 0, 0),
                                   memory_space=pltpu.SMEM),
                      pl.BlockSpec(memory_space=pl.ANY)],
            out_specs=pl.BlockSpec(memory_space=pl.ANY),
            scratch_shapes=[pltpu.VMEM((SUB * slab_rows, LANES), jnp.uint32),
                            pltpu.SemaphoreType.DMA(()),
                            pltpu.SemaphoreType.DMA(())]),
        out_shape=jax.ShapeDtypeStruct((p_rows * slab_rows, LANES), jnp.uint32),
        compiler_params=_cparams(("arbitrary",)),
        name="dispatch",
    )(zstart, tail, dest3, hp)


def _expert_kernel(vtile_ref, vexp_ref, cmask_ref, zmask_ref, first_ref,
                   xs_ref, wg_ref, wu_ref, wd_ref, bg_ref, bu_ref, bd_ref, ys_ref,
                   xb, acc, wgb, wub, wdb, *, n_f):
    del vtile_ref, vexp_ref
    v = pl.program_id(0)
    j = pl.program_id(1)
    tm, d = xb.shape
    slab_rows = d // (2 * LANES)
    f32, bf16 = jnp.float32, jnp.bfloat16
    cm = cmask_ref[v]
    zm = zmask_ref[v]

    @pl.when((j == 0) & (first_ref[v] == 1))
    def _():
        for jj in range(slab_rows):
            lo, hi = _unpack_slab_words(xs_ref[pl.ds(jj, tm, stride=slab_rows), :])
            xb[:, jj * LANES:(jj + 1) * LANES] = lo.astype(bf16)
            xb[:, d // 2 + jj * LANES:d // 2 + (jj + 1) * LANES] = hi.astype(bf16)

    @pl.when(cm != 0)
    def _():
        wgb[...] = wg_ref[...].astype(bf16)
        wub[...] = wu_ref[...].astype(bf16)
        wdb[...] = wd_ref[...].astype(bf16)

    for s in range(tm // SUB):
        rows = slice(s * SUB, (s + 1) * SUB)
        owned = ((cm >> s) & 1) == 1

        @pl.when(owned & (j == 0))
        def _():
            acc[rows, :] = jnp.zeros((SUB, d), f32)

        @pl.when(owned)
        def _():
            xt = xb[rows, :]
            g = jnp.dot(xt, wgb[...], preferred_element_type=f32) + bg_ref[...]
            u = jnp.dot(xt, wub[...], preferred_element_type=f32) + bu_ref[...]
            g = jnp.minimum(g, SWIGLU_LIMIT)
            u = jnp.clip(u, -SWIGLU_LIMIT, SWIGLU_LIMIT)
            a = (u + 1.0) * (g * jax.nn.sigmoid(SWIGLU_ALPHA * g))
            acc[rows, :] += jnp.dot(a.astype(bf16), wdb[...], preferred_element_type=f32)

        @pl.when(owned & (j == n_f - 1))
        def _():
            words = _pack_slab_words(acc[rows, :] + bd_ref[...])
            for jj in range(slab_rows):
                ys_ref[pl.ds(s * SUB * slab_rows + jj, SUB, stride=slab_rows), :] = (
                    words[:, jj * LANES:(jj + 1) * LANES])

        @pl.when((((zm >> s) & 1) == 1) & (j == n_f - 1))
        def _():
            ys_ref[s * SUB * slab_rows:(s + 1) * SUB * slab_rows, :] = jnp.zeros(
                (SUB * slab_rows, LANES), jnp.uint32)


def _expert(xs2, w_gate, b_gate, w_up, b_up, w_down, b_down, vtile, vexp, cmask, zmask, first,
            n_tiles):
    n_experts, d, f = w_gate.shape
    tm, tf = EXPERT_TM, min(EXPERT_TF, f)
    assert f % tf == 0 and tm % SUB == 0
    n_f = f // tf
    slab_rows = d // (2 * LANES)
    n_visits = vtile.shape[0]

    def f_idx(v, j, cm):
        return jnp.where(cm[v] == 0, n_f - 1, j)

    x_map = lambda v, j, vt, ve, cm, zm, fs: (vt[v], 0)
    kernel = functools.partial(_expert_kernel, n_f=n_f)
    return pl.pallas_call(
        kernel,
        grid_spec=pltpu.PrefetchScalarGridSpec(
            num_scalar_prefetch=5,
            grid=(n_visits, n_f),
            in_specs=[
                pl.BlockSpec((tm * slab_rows, LANES), x_map),
                pl.BlockSpec((None, d, tf), lambda v, j, vt, ve, cm, zm, fs: (ve[v], 0, f_idx(v, j, cm))),
                pl.BlockSpec((None, d, tf), lambda v, j, vt, ve, cm, zm, fs: (ve[v], 0, f_idx(v, j, cm))),
                pl.BlockSpec((None, tf, d), lambda v, j, vt, ve, cm, zm, fs: (ve[v], f_idx(v, j, cm), 0)),
                pl.BlockSpec((None, 1, tf), lambda v, j, vt, ve, cm, zm, fs: (ve[v], 0, f_idx(v, j, cm))),
                pl.BlockSpec((None, 1, tf), lambda v, j, vt, ve, cm, zm, fs: (ve[v], 0, f_idx(v, j, cm))),
                pl.BlockSpec((None, 1, d), lambda v, j, vt, ve, cm, zm, fs: (ve[v], 0, 0)),
            ],
            out_specs=pl.BlockSpec((tm * slab_rows, LANES), x_map),
            scratch_shapes=[pltpu.VMEM((tm, d), jnp.bfloat16),
                            pltpu.VMEM((tm, d), jnp.float32),
                            pltpu.VMEM((d, tf), jnp.bfloat16),
                            pltpu.VMEM((d, tf), jnp.bfloat16),
                            pltpu.VMEM((tf, d), jnp.bfloat16)]),
        out_shape=jax.ShapeDtypeStruct((n_tiles * tm * slab_rows, LANES), jnp.uint32),
        compiler_params=_cparams(("arbitrary", "arbitrary")),
        name="expert",
    )(vtile, vexp, cmask, zmask, first, xs2, w_gate, w_up, w_down,
      b_gate.reshape(n_experts, 1, f), b_up.reshape(n_experts, 1, f),
      b_down.reshape(n_experts, 1, d))


def _combine_kernel(dest_ref, ys_ref, wgt_ref, x1_ref, mod_ref, g_ref, o_ref, buf, ybuf, sem,
                    *, tm):
    d = x1_ref.shape[1]
    slab_rows = d // (2 * LANES)

    def body(t, carry):
        for k in range(TOP_K):
            src = _slab(ys_ref, dest_ref[0, 0, t * TOP_K + k], slab_rows)
            pltpu.make_async_copy(src, _slab(buf, k * tm + t, slab_rows), sem).start()
        return carry
    lax.fori_loop(0, tm, body, 0)
    pltpu.make_async_copy(_slab(ys_ref, 0, slab_rows, tm * TOP_K), buf, sem).wait()

    wgt = wgt_ref[...]
    for jj in range(slab_rows):
        ylo = jnp.zeros((tm, LANES), jnp.float32)
        yhi = jnp.zeros((tm, LANES), jnp.float32)
        for k in range(TOP_K):
            lo, hi = _unpack_slab_words(buf[pl.ds(k * tm * slab_rows + jj, tm, stride=slab_rows), :])
            ylo = ylo + wgt[:, k:k + 1] * lo
            yhi = yhi + wgt[:, k:k + 1] * hi
        ybuf[:, jj * LANES:(jj + 1) * LANES] = ylo
        ybuf[:, d // 2 + jj * LANES:d // 2 + (jj + 1) * LANES] = yhi
    mod = mod_ref[0]
    o_ref[...] = x1_ref[...] + mod[5:6] * _rms(ybuf[...], g_ref[...])


def _combine(ys, dest_flat, wgt, x1, mod3, seq, g_post):
    n, d = x1.shape
    slab_rows = d // (2 * LANES)
    tm = COMBINE_TM
    assert seq % tm == 0
    tiles_per_seq = seq // tm
    dest3 = dest_flat.reshape(n // tm, 1, tm * TOP_K)
    kernel = functools.partial(_combine_kernel, tm=tm)
    return pl.pallas_call(
        kernel,
        grid=(n // tm,),
        in_specs=[pl.BlockSpec((1, 1, tm * TOP_K), lambda i: (i, 0, 0), memory_space=pltpu.SMEM),
                  pl.BlockSpec(memory_space=pl.ANY),
                  pl.BlockSpec((tm, LANES), lambda i: (i, 0)),
                  pl.BlockSpec((tm, d), lambda i: (i, 0)),
                  pl.BlockSpec((1, mod3.shape[1], d), lambda i: (i // tiles_per_seq, 0, 0)),
                  pl.BlockSpec((1, d), lambda i: (0, 0))],
        out_specs=pl.BlockSpec((tm, d), lambda i: (i, 0)),
        out_shape=jax.ShapeDtypeStruct((n, d), jnp.float32),
        scratch_shapes=[pltpu.VMEM((TOP_K * tm * slab_rows, LANES), jnp.uint32),
                        pltpu.VMEM((tm, d), jnp.float32),
                        pltpu.SemaphoreType.DMA(())],
        compiler_params=_cparams(("arbitrary",)),
        name="combine",
    )(dest3, ys, wgt, x1, mod3, g_post)


def _routing_tables(counts, n_assign):
    n_experts = counts.shape[0]
    sub_per_tile = EXPERT_TM // SUB
    p_rows = -(-(n_assign + n_experts * SUB) // EXPERT_TM) * EXPERT_TM
    n_tiles = p_rows // EXPERT_TM
    n_sub = p_rows // SUB
    n_visits = n_tiles + n_experts

    padded = jnp.maximum((counts + SUB - 1) // SUB, 1) * SUB
    pad_end = jnp.cumsum(padded)
    pad_start = pad_end - padded
    total = pad_end[-1]
    zstart = (pad_start + counts).astype(jnp.int32)
    tail = jnp.stack([total, (p_rows + SUB - total) // SUB]).astype(jnp.int32)

    sub = jnp.arange(n_sub, dtype=jnp.int32)
    pos = sub * SUB
    owner = jnp.searchsorted(pad_end, pos, side="right").astype(jnp.int32)
    owner = jnp.where(pos < total, jnp.minimum(owner, n_experts - 1), n_experts)
    prev_owner = jnp.concatenate([jnp.full((1,), -1, jnp.int32), owner[:-1]])
    new_visit = (sub % sub_per_tile == 0) | (owner != prev_owner)
    visit = jnp.cumsum(new_visit.astype(jnp.int32)) - 1
    bit = jnp.left_shift(1, sub % sub_per_tile).astype(jnp.int32)

    vtile = jnp.full((n_visits,), n_tiles - 1, jnp.int32).at[visit].set(sub // sub_per_tile)
    vown = jnp.full((n_visits,), n_experts, jnp.int32).at[visit].set(owner)
    bits = jnp.zeros((n_visits,), jnp.int32).at[visit].add(bit)
    first = jnp.zeros((n_visits,), jnp.int32).at[visit[::sub_per_tile]].set(1)
    cmask = jnp.where(vown < n_experts, bits, 0)
    zmask = jnp.where(vown < n_experts, 0, bits)
    vexp = jnp.minimum(vown, n_experts - 1)
    return pad_start.astype(jnp.int32), zstart, tail, vtile, vexp, cmask, zmask, first, p_rows, n_tiles


def kernel(x, c, w_ada, b_ada, g_pre_mix, g_post_mix, w_in, conv_a_w, conv_b_w, conv_b_b, ln_b_g, ln_b_b, w_out, g_pre_ffn, g_post_ffn, w_router, b_router, w_gate, b_gate, w_up, b_up, w_down, b_down):
    bsz, seq, d = x.shape
    n = bsz * seq
    depth = w_ada.shape[0]
    n_experts = w_router.shape[-1]
    assert n_experts <= LANES
    xf = x.reshape(n, d)
    for l in range(depth):
        mod = _ada(c, w_ada[l], b_ada[l])
        n_mod = mod.shape[1] // d
        mod3 = mod.reshape(bsz, n_mod, d)

        wr = jnp.zeros((d, LANES), jnp.float32).at[:, :n_experts].set(w_router[l])
        wr_hi = wr.astype(jnp.bfloat16)
        wr_lo = (wr - wr_hi.astype(jnp.float32)).astype(jnp.bfloat16)
        br = jnp.full((1, LANES), NEG_BIG, jnp.float32).at[0, :n_experts].set(b_router[l])

        x1, hp, idx, wgt, rank, cnt = _mix(
            xf, mod3, seq, g_pre_mix[l].reshape(1, d), g_post_mix[l].reshape(1, d),
            g_pre_ffn[l].reshape(1, d), w_in[l].astype(jnp.bfloat16), conv_a_w[l], conv_b_w[l],
            conv_b_b[l].reshape(1, -1), ln_b_g[l].reshape(1, -1), ln_b_b[l].reshape(1, -1),
            w_out[l].astype(jnp.bfloat16), wr_hi, wr_lo, br)

        (pad_start, zstart, tail, vtile, vexp, cmask, zmask, first, p_rows,
         n_tiles) = _routing_tables(cnt[0, :n_experts], n * TOP_K)
        start_row = jnp.zeros((1, LANES), jnp.float32).at[0, :n_experts].set(
            pad_start.astype(jnp.float32))
        dest = _plan(idx, rank, start_row)
        dest_flat = dest[:, :TOP_K].reshape(n * TOP_K)

        slab_rows = d // (2 * LANES)
        xs = _dispatch(hp, dest_flat, zstart, tail, p_rows + SUB, slab_rows)
        ys = _expert(xs, w_gate[l], b_gate[l], w_up[l], b_up[l], w_down[l], b_down[l],
                     vtile, vexp, cmask, zmask, first, n_tiles)
        xf = _combine(ys, dest_flat, wgt, x1, mod3, seq, g_post_ffn[l].reshape(1, d))
    return xf.reshape(bsz, seq, d)
```

```python
import functools

import jax
import jax.numpy as jnp
from jax import lax
from jax.experimental import pallas as pl
from jax.experimental.pallas import tpu as pltpu

EPS = 1e-6
TOP_K = 4
CONV_A = 3
CONV_B = 31
SWIGLU_ALPHA = 1.702
SWIGLU_LIMIT = 7.0

LANES = 128
SUBLANES = 8
VMEM_LIMIT_BYTES = 56 * 1024 * 1024

ADA_TN = 1024
MIX_TM = 256
HALO_A = 8
HALO_B = 32
PLAN_TM = 2048
DISPATCH_TM = 1024
DISPATCH_UNROLL = 8
SUB = 256
EXPERT_TM = 1024
EXPERT_TF = 256
COMBINE_TM = 256
COMBINE_UNROLL = 8
NEG_BIG = -1e30


def _cparams(sem):
    return pltpu.CompilerParams(dimension_semantics=sem, vmem_limit_bytes=VMEM_LIMIT_BYTES)


def _rms(x, g):
    return x * lax.rsqrt(jnp.mean(x * x, axis=-1, keepdims=True) + EPS) * g


def _pack_slab_words(v):
    half = v.shape[1] // 2
    lo = pltpu.bitcast(v[:, :half].astype(jnp.bfloat16).astype(jnp.float32), jnp.uint32)
    hi = pltpu.bitcast(v[:, half:].astype(jnp.bfloat16).astype(jnp.float32), jnp.uint32)
    return (lo >> 16) | (hi & jnp.uint32(0xFFFF0000))


def _unpack_slab_words(w):
    lo = pltpu.bitcast(w << 16, jnp.float32)
    hi = pltpu.bitcast(w & jnp.uint32(0xFFFF0000), jnp.float32)
    return lo, hi


def _ada_kernel(c_ref, w_ref, b_ref, o_ref):
    c = c_ref[...]
    ca = (c * jax.nn.sigmoid(c)).astype(jnp.bfloat16)
    o_ref[...] = jnp.dot(ca, w_ref[...].astype(jnp.bfloat16),
                         preferred_element_type=jnp.float32) + b_ref[...]


def _ada(c, w_ada, b_ada):
    bsz, d = c.shape
    n_out = w_ada.shape[1]
    tn = min(ADA_TN, n_out)
    assert n_out % tn == 0 and bsz <= SUBLANES
    c8 = jnp.zeros((SUBLANES, d), jnp.float32).at[:bsz].set(c)
    out = pl.pallas_call(
        _ada_kernel,
        grid=(n_out // tn,),
        in_specs=[pl.BlockSpec((SUBLANES, d), lambda j: (0, 0)),
                  pl.BlockSpec((d, tn), lambda j: (0, j)),
                  pl.BlockSpec((1, tn), lambda j: (0, j))],
        out_specs=pl.BlockSpec((SUBLANES, tn), lambda j: (0, j)),
        out_shape=jax.ShapeDtypeStruct((SUBLANES, n_out), jnp.float32),
        compiler_params=_cparams(("arbitrary",)),
        name="ada",
    )(c8, w_ada, b_ada.reshape(1, n_out))
    return out[:bsz]


def _mix_kernel(x_ref, mod_ref, gpre_ref, gpost_ref, gffn_ref, win_ref, caw_ref, cbw_ref,
                cbb_ref, lng_ref, lnb_ref, wout_ref, wrh_ref, wrl_ref, br_ref,
                x1_ref, hp_ref, idx_ref, wgt_ref, rank_ref, cnt_ref,
                pa_buf, u_buf, v_buf, ycat, run_cnt, *, tiles_per_seq, wa, wb):
    i = pl.program_id(0)
    tm, d = x_ref.shape
    f32, bf16 = jnp.float32, jnp.bfloat16

    @pl.when(i == 0)
    def _():
        run_cnt[...] = jnp.zeros_like(run_cnt)

    @pl.when(i % tiles_per_seq == 0)
    def _():
        pa_buf[:, 0:HALO_A, :] = jnp.zeros((wa // LANES, HALO_A, LANES), f32)
        u_buf[:, 0:HALO_B, :] = jnp.zeros((wb // LANES, HALO_B, LANES), f32)

    x = x_ref[...]
    mod = mod_ref[0]
    h = (_rms(x, gpre_ref[...]) * (1.0 + mod[1:2]) + mod[0:1]).astype(bf16)

    proj_a = jnp.dot(h, win_ref[:, 0:3 * wa], preferred_element_type=f32)
    for cb in range(wa // LANES):
        sl = slice(cb * LANES, (cb + 1) * LANES)
        pa = proj_a[:, sl] * proj_a[:, 2 * wa + cb * LANES:2 * wa + (cb + 1) * LANES]
        pa_buf[cb, HALO_A:HALO_A + tm, :] = pa
        conv = caw_ref[CONV_A - 1:CONV_A, sl] * pa
        for k in range(CONV_A - 1):
            off = HALO_A - (CONV_A - 1) + k
            conv = conv + caw_ref[k:k + 1, sl] * pa_buf[cb, off:off + tm, :]
        ycat[:, sl] = (proj_a[:, wa + cb * LANES:wa + (cb + 1) * LANES] * conv).astype(bf16)
        pa_buf[cb, 0:HALO_A, :] = pa_buf[cb, tm:tm + HALO_A, :]

    proj_b = jnp.dot(h, win_ref[:, 3 * wa:3 * wa + 2 * wb], preferred_element_type=f32)
    for cb in range(wb // LANES):
        sl = slice(cb * LANES, (cb + 1) * LANES)
        u = proj_b[:, sl] * jax.nn.sigmoid(proj_b[:, wb + cb * LANES:wb + (cb + 1) * LANES])
        u_buf[cb, HALO_B:HALO_B + tm, :] = u
        acc = cbb_ref[:, sl] + cbw_ref[CONV_B - 1:CONV_B, sl] * u
        for k in range(CONV_B - 1):
            off = HALO_B - (CONV_B - 1) + k
            acc = acc + cbw_ref[k:k + 1, sl] * u_buf[cb, off:off + tm, :]
        v_buf[:, sl] = acc
        u_buf[cb, 0:HALO_B, :] = u_buf[cb, tm:tm + HALO_B, :]
    v = v_buf[...]
    mu = jnp.mean(v, axis=-1, keepdims=True)
    vc = v - mu
    var = jnp.mean(vc * vc, axis=-1, keepdims=True)
    yb = vc * lax.rsqrt(var + EPS) * lng_ref[...] + lnb_ref[...]
    ycat[:, wa:wa + wb] = (yb * jax.nn.sigmoid(yb)).astype(bf16)

    y = jnp.dot(ycat[...], wout_ref[...], preferred_element_type=f32)
    x1 = x + mod[2:3] * _rms(y, gpost_ref[...])
    x1_ref[...] = x1

    h2 = _rms(x1, gffn_ref[...]) * (1.0 + mod[4:5]) + mod[3:4]
    words = _pack_slab_words(h2)
    slab_rows = d // (2 * LANES)
    for j in range(slab_rows):
        hp_ref[pl.ds(j, tm, stride=slab_rows), :] = words[:, j * LANES:(j + 1) * LANES]

    h2_hi = h2.astype(bf16)
    h2_lo = (h2 - h2_hi.astype(f32)).astype(bf16)
    logits = (jnp.dot(h2_hi, wrh_ref[...], preferred_element_type=f32)
              + jnp.dot(h2_hi, wrl_ref[...], preferred_element_type=f32)
              + jnp.dot(h2_lo, wrh_ref[...], preferred_element_type=f32)
              + br_ref[...])

    lane = lax.broadcasted_iota(jnp.int32, (tm, LANES), 1)
    lane_f = lane.astype(f32)
    vals, idxs = [], []
    cur = logits
    for _ in range(TOP_K):
        m = jnp.max(cur, axis=-1, keepdims=True)
        ix = jnp.min(jnp.where(cur == m, lane_f, float(LANES)), axis=-1,
                     keepdims=True).astype(jnp.int32)
        vals.append(m)
        idxs.append(ix)
        cur = jnp.where(lane == ix, -jnp.inf, cur)
    exps = [jnp.exp(vk - vals[0]) for vk in vals]
    denom = exps[0]
    for ek in exps[1:]:
        denom = denom + ek
    inv = 1.0 / denom

    row = lax.broadcasted_iota(jnp.int32, (tm, tm), 0)
    col = lax.broadcasted_iota(jnp.int32, (tm, tm), 1)
    tri = jnp.where(col < row, 1.0, 0.0).astype(bf16)
    run = run_cnt[...]
    idx_out = jnp.zeros((tm, LANES), jnp.int32)
    wgt_out = jnp.zeros((tm, LANES), f32)
    rank_out = jnp.zeros((tm, LANES), f32)
    for k in range(TOP_K):
        oh = jnp.where(lane == idxs[k], 1.0, 0.0)
        before = jnp.dot(tri, oh.astype(bf16), preferred_element_type=f32)
        rank_k = jnp.sum(oh * (before + run), axis=-1, keepdims=True)
        run = run + jnp.sum(oh, axis=0, keepdims=True)
        idx_out = jnp.where(lane == k, idxs[k], idx_out)
        wgt_out = jnp.where(lane == k, exps[k] * inv, wgt_out)
        rank_out = jnp.where(lane == k, rank_k, rank_out)
    run_cnt[...] = run
    idx_ref[...] = idx_out
    wgt_ref[...] = wgt_out
    rank_ref[...] = rank_out.astype(jnp.int32)
    cnt_ref[...] = jnp.broadcast_to(run, cnt_ref.shape).astype(jnp.int32)


def _mix(x2, mod3, seq, g_pre, g_post, g_ffn, w_in, conv_a_w, conv_b_w, conv_b_b, ln_g, ln_b,
         w_out, wr_hi, wr_lo, br):
    n, d = x2.shape
    wa = conv_a_w.shape[1]
    wb = conv_b_w.shape[1]
    tm = MIX_TM
    assert seq % tm == 0 and tm >= HALO_B and wa % LANES == 0 and wb % LANES == 0
    assert wa + wb == d and d % (2 * LANES) == 0
    tiles_per_seq = seq // tm
    slab_rows = d // (2 * LANES)
    const = lambda shape: pl.BlockSpec(shape, lambda i: (0,) * len(shape))
    resident = lambda shape: pl.BlockSpec(shape, lambda i: (0,) * len(shape),
                                          pipeline_mode=pl.Buffered(1))
    row_block = lambda w: pl.BlockSpec((tm, w), lambda i: (i, 0))
    kernel = functools.partial(_mix_kernel, tiles_per_seq=tiles_per_seq, wa=wa, wb=wb)
    return pl.pallas_call(
        kernel,
        grid=(n // tm,),
        in_specs=[row_block(d),
                  pl.BlockSpec((1, mod3.shape[1], d), lambda i: (i // tiles_per_seq, 0, 0)),
                  const((1, d)), const((1, d)), const((1, d)),
                  resident(w_in.shape),
                  const(conv_a_w.shape), const(conv_b_w.shape),
                  const((1, wb)), const((1, wb)), const((1, wb)),
                  resident(w_out.shape),
                  const(wr_hi.shape), const(wr_lo.shape), const((1, LANES))],
        out_specs=[row_block(d),
                   pl.BlockSpec((tm * slab_rows, LANES), lambda i: (i, 0)),
                   row_block(LANES), row_block(LANES), row_block(LANES),
                   const((SUBLANES, LANES))],
        out_shape=[jax.ShapeDtypeStruct((n, d), jnp.float32),
                   jax.ShapeDtypeStruct((n * slab_rows, LANES), jnp.uint32),
                   jax.ShapeDtypeStruct((n, LANES), jnp.int32),
                   jax.ShapeDtypeStruct((n, LANES), jnp.float32),
                   jax.ShapeDtypeStruct((n, LANES), jnp.int32),
                   jax.ShapeDtypeStruct((SUBLANES, LANES), jnp.int32)],
        scratch_shapes=[pltpu.VMEM((wa // LANES, tm + HALO_A, LANES), jnp.float32),
                        pltpu.VMEM((wb // LANES, tm + HALO_B, LANES), jnp.float32),
                        pltpu.VMEM((tm, wb), jnp.float32),
                        pltpu.VMEM((tm, d), jnp.bfloat16),
                        pltpu.VMEM((1, LANES), jnp.float32)],
        compiler_params=_cparams(("arbitrary",)),
        name="mix",
    )(x2, mod3, g_pre, g_post, g_ffn, w_in, conv_a_w, conv_b_w, conv_b_b, ln_g, ln_b,
      w_out, wr_hi, wr_lo, br)


def _plan_kernel(idx_ref, rank_ref, start_ref, dest_ref):
    tm = idx_ref.shape[0]
    lane = lax.broadcasted_iota(jnp.int32, (tm, LANES), 1)
    idx = idx_ref[...]
    rank = rank_ref[...]
    start = start_ref[...]
    out = jnp.zeros((tm, LANES), jnp.int32)
    for k in range(TOP_K):
        base = jnp.sum(jnp.where(lane == idx[:, k:k + 1], start, 0.0), axis=-1, keepdims=True)
        out = jnp.where(lane == k, base.astype(jnp.int32) + rank[:, k:k + 1], out)
    dest_ref[...] = out


def _plan(idx, rank, start_row):
    n = idx.shape[0]
    tm = min(PLAN_TM, n)
    assert n % tm == 0
    blk = pl.BlockSpec((tm, LANES), lambda i: (i, 0))
    return pl.pallas_call(
        _plan_kernel,
        grid=(n // tm,),
        in_specs=[blk, blk, pl.BlockSpec((1, LANES), lambda i: (0, 0))],
        out_specs=blk,
        out_shape=jax.ShapeDtypeStruct((n, LANES), jnp.int32),
        compiler_params=_cparams(("arbitrary",)),
        name="plan",
    )(idx, rank, start_row)


def _slab(ref, row, slab_rows, count=1):
    return ref.at[pl.ds(pl.multiple_of(row * slab_rows, slab_rows), count * slab_rows)]


def _dispatch_kernel(zstart_ref, tail_ref, dest_ref, hp_ref, xs_ref, zbuf, sem, zsem,
                     *, tm, n_experts, slab_rows):
    i = pl.program_id(0)

    def zero_copy(start):
        return pltpu.make_async_copy(zbuf, _slab(xs_ref, start, slab_rows, SUB), zsem)

    @pl.when(i == 0)
    def _():
        zbuf[...] = jnp.zeros_like(zbuf)
        for parity in range(2):
            for e in range(parity, n_experts, 2):
                zero_copy(zstart_ref[e]).start()
            for e in range(parity, n_experts, 2):
                zero_copy(zstart_ref[e]).wait()

        def tail_start(t, carry):
            zero_copy(tail_ref[0] + t * SUB).start()
            return carry

        def tail_wait(t, carry):
            zero_copy(tail_ref[0] + t * SUB).wait()
            return carry
        lax.fori_loop(0, tail_ref[1], tail_start, 0)
        lax.fori_loop(0, tail_ref[1], tail_wait, 0)

    def body(t, carry):
        for u in range(DISPATCH_UNROLL):
            tok = t * DISPATCH_UNROLL + u
            src = _slab(hp_ref, tok, slab_rows)
            for k in range(TOP_K):
                dst = _slab(xs_ref, dest_ref[0, 0, tok * TOP_K + k], slab_rows)
                pltpu.make_async_copy(src, dst, sem).start()
        return carry
    lax.fori_loop(0, tm // DISPATCH_UNROLL, body, 0)
    for k in range(TOP_K):
        pltpu.make_async_copy(hp_ref, _slab(xs_ref, 0, slab_rows, tm), sem).wait()


def _dispatch(hp, dest_flat, zstart, tail, p_rows, slab_rows):
    n = hp.shape[0] // slab_rows
    tm = min(DISPATCH_TM, n)
    assert n % tm == 0 and tm % DISPATCH_UNROLL == 0
    n_experts = zstart.shape[0]
    dest3 = dest_flat.reshape(n // tm, 1, tm * TOP_K)
    kernel = functools.partial(_dispatch_kernel, tm=tm, n_experts=n_experts, slab_rows=slab_rows)
    return pl.pallas_call(
        kernel,
        grid_spec=pltpu.PrefetchScalarGridSpec(
            num_scalar_prefetch=2,
            grid=(n // tm,),
            in_specs=[pl.BlockSpec((1, 1, tm * TOP_K), lambda i, *_: (i, 0, 0),
                                   memory_space=pltpu.SMEM),
                      pl.BlockSpec((tm * slab_rows, LANES), lambda i, *_: (i, 0))],
            out_specs=pl.BlockSpec(memory_space=pl.ANY),
            scratch_shapes=[pltpu.VMEM((SUB * slab_rows, LANES), jnp.uint32),
                            pltpu.SemaphoreType.DMA(()),
                            pltpu.SemaphoreType.DMA(())]),
        out_shape=jax.ShapeDtypeStruct((p_rows * slab_rows, LANES), jnp.uint32),
        compiler_params=_cparams(("arbitrary",)),
        name="dispatch",
    )(zstart, tail, dest3, hp)


def _expert_kernel(vtile_ref, vexp_ref, cmask_ref, zmask_ref, first_ref,
                   xs_ref, wg_ref, wu_ref, wd_ref, bg_ref, bu_ref, bd_ref, ys_ref,
                   xb, acc, wgb, wub, wdb, *, n_f):
    del vtile_ref, vexp_ref
    v = pl.program_id(0)
    j = pl.program_id(1)
    tm, d = xb.shape
    slab_rows = d // (2 * LANES)
    f32, bf16 = jnp.float32, jnp.bfloat16
    cm = cmask_ref[v]
    zm = zmask_ref[v]

    @pl.when((j == 0) & (first_ref[v] == 1))
    def _():
        for jj in range(slab_rows):
            lo, hi = _unpack_slab_words(xs_ref[pl.ds(jj, tm, stride=slab_rows), :])
            xb[:, jj * LANES:(jj + 1) * LANES] = lo.astype(bf16)
            xb[:, d // 2 + jj * LANES:d // 2 + (jj + 1) * LANES] = hi.astype(bf16)

    @pl.when(cm != 0)
    def _():
        wgb[...] = wg_ref[...].astype(bf16)
        wub[...] = wu_ref[...].astype(bf16)
        wdb[...] = wd_ref[...].astype(bf16)

    for s in range(tm // SUB):
        rows = slice(s * SUB, (s + 1) * SUB)
        owned = ((cm >> s) & 1) == 1

        @pl.when(owned & (j == 0))
        def _():
            acc[rows, :] = jnp.zeros((SUB, d), f32)

        @pl.when(owned)
        def _():
            xt = xb[rows, :]
            g = jnp.dot(xt, wgb[...], preferred_element_type=f32) + bg_ref[...]
            u = jnp.dot(xt, wub[...], preferred_element_type=f32) + bu_ref[...]
            g = jnp.minimum(g, SWIGLU_LIMIT)
            u = jnp.clip(u, -SWIGLU_LIMIT, SWIGLU_LIMIT)
            a = (u + 1.0) * (g * jax.nn.sigmoid(SWIGLU_ALPHA * g))
            acc[rows, :] += jnp.dot(a.astype(bf16), wdb[...], preferred_element_type=f32)

        @pl.when(owned & (j == n_f - 1))
        def _():
            words = _pack_slab_words(acc[rows, :] + bd_ref[...])
            for jj in range(slab_rows):
                ys_ref[pl.ds(s * SUB * slab_rows + jj, SUB, stride=slab_rows), :] = (
                    words[:, jj * LANES:(jj + 1) * LANES])

        @pl.when((((zm >> s) & 1) == 1) & (j == n_f - 1))
        def _():
            ys_ref[s * SUB * slab_rows:(s + 1) * SUB * slab_rows, :] = jnp.zeros(
                (SUB * slab_rows, LANES), jnp.uint32)


def _expert(xs2, w_gate, b_gate, w_up, b_up, w_down, b_down, vtile, vexp, cmask, zmask, first,
            n_tiles):
    n_experts, d, f = w_gate.shape
    tm, tf = EXPERT_TM, min(EXPERT_TF, f)
    assert f % tf == 0 and tm % SUB == 0
    n_f = f // tf
    slab_rows = d // (2 * LANES)
    n_visits = vtile.shape[0]

    def f_idx(v, j, cm):
        return jnp.where(cm[v] == 0, n_f - 1, j)

    x_map = lambda v, j, vt, ve, cm, zm, fs: (vt[v], 0)
    kernel = functools.partial(_expert_kernel, n_f=n_f)
    return pl.pallas_call(
        kernel,
        grid_spec=pltpu.PrefetchScalarGridSpec(
            num_scalar_prefetch=5,
            grid=(n_visits, n_f),
            in_specs=[
                pl.BlockSpec((tm * slab_rows, LANES), x_map),
                pl.BlockSpec((None, d, tf), lambda v, j, vt, ve, cm, zm, fs: (ve[v], 0, f_idx(v, j, cm))),
                pl.BlockSpec((None, d, tf), lambda v, j, vt, ve, cm, zm, fs: (ve[v], 0, f_idx(v, j, cm))),
                pl.BlockSpec((None, tf, d), lambda v, j, vt, ve, cm, zm, fs: (ve[v], f_idx(v, j, cm), 0)),
                pl.BlockSpec((None, 1, tf), lambda v, j, vt, ve, cm, zm, fs: (ve[v], 0, f_idx(v, j, cm))),
                pl.BlockSpec((None, 1, tf), lambda v, j, vt, ve, cm, zm, fs: (ve[v], 0, f_idx(v, j, cm))),
                pl.BlockSpec((None, 1, d), lambda v, j, vt, ve, cm, zm, fs: (ve[v], 0, 0)),
            ],
            out_specs=pl.BlockSpec((tm * slab_rows, LANES), x_map),
            scratch_shapes=[pltpu.VMEM((tm, d), jnp.bfloat16),
                            pltpu.VMEM((tm, d), jnp.float32),
                            pltpu.VMEM((d, tf), jnp.bfloat16),
                            pltpu.VMEM((d, tf), jnp.bfloat16),
                            pltpu.VMEM((tf, d), jnp.bfloat16)]),
        out_shape=jax.ShapeDtypeStruct((n_tiles * tm * slab_rows, LANES), jnp.uint32),
        compiler_params=_cparams(("arbitrary", "arbitrary")),
        name="expert",
    )(vtile, vexp, cmask, zmask, first, xs2, w_gate, w_up, w_down,
      b_gate.reshape(n_experts, 1, f), b_up.reshape(n_experts, 1, f),
      b_down.reshape(n_experts, 1, d))


def _combine_kernel(dest_ref, ys_ref, wgt_ref, x1_ref, mod_ref, g_ref, o_ref, buf, ybuf, sem,
                    *, tm):
    d = x1_ref.shape[1]
    slab_rows = d // (2 * LANES)

    def body(t, carry):
        for u in range(COMBINE_UNROLL):
            tok = t * COMBINE_UNROLL + u
            for k in range(TOP_K):
                src = _slab(ys_ref, dest_ref[0, 0, tok * TOP_K + k], slab_rows)
                pltpu.make_async_copy(src, _slab(buf, k * tm + tok, slab_rows), sem).start()
        return carry
    lax.fori_loop(0, tm // COMBINE_UNROLL, body, 0)
    pltpu.make_async_copy(_slab(ys_ref, 0, slab_rows, tm * TOP_K), buf, sem).wait()

    wgt = wgt_ref[...]
    for jj in range(slab_rows):
        ylo = jnp.zeros((tm, LANES), jnp.float32)
        yhi = jnp.zeros((tm, LANES), jnp.float32)
        for k in range(TOP_K):
            lo, hi = _unpack_slab_words(buf[pl.ds(k * tm * slab_rows + jj, tm, stride=slab_rows), :])
            ylo = ylo + wgt[:, k:k + 1] * lo
            yhi = yhi + wgt[:, k:k + 1] * hi
        ybuf[:, jj * LANES:(jj + 1) * LANES] = ylo
        ybuf[:, d // 2 + jj * LANES:d // 2 + (jj + 1) * LANES] = yhi
    mod = mod_ref[0]
    o_ref[...] = x1_ref[...] + mod[5:6] * _rms(ybuf[...], g_ref[...])


def _combine(ys, dest_flat, wgt, x1, mod3, seq, g_post):
    n, d = x1.shape
    slab_rows = d // (2 * LANES)
    tm = COMBINE_TM
    assert seq % tm == 0
    tiles_per_seq = seq // tm
    dest3 = dest_flat.reshape(n // tm, 1, tm * TOP_K)
    kernel = functools.partial(_combine_kernel, tm=tm)
    return pl.pallas_call(
        kernel,
        grid=(n // tm,),
        in_specs=[pl.BlockSpec((1, 1, tm * TOP_K), lambda i: (i, 0, 0), memory_space=pltpu.SMEM),
                  pl.BlockSpec(memory_space=pl.ANY),
                  pl.BlockSpec((tm, LANES), lambda i: (i, 0)),
                  pl.BlockSpec((tm, d), lambda i: (i, 0)),
                  pl.BlockSpec((1, mod3.shape[1], d), lambda i: (i // tiles_per_seq, 0, 0)),
                  pl.BlockSpec((1, d), lambda i: (0, 0))],
        out_specs=pl.BlockSpec((tm, d), lambda i: (i, 0)),
        out_shape=jax.ShapeDtypeStruct((n, d), jnp.float32),
        scratch_shapes=[pltpu.VMEM((TOP_K * tm * slab_rows, LANES), jnp.uint32),
                        pltpu.VMEM((tm, d), jnp.float32),
                        pltpu.SemaphoreType.DMA(())],
        compiler_params=_cparams(("arbitrary",)),
        name="combine",
    )(dest3, ys, wgt, x1, mod3, g_post)


def _routing_tables(counts, n_assign):
    n_experts = counts.shape[0]
    sub_per_tile = EXPERT_TM // SUB
    p_rows = -(-(n_assign + n_experts * SUB) // EXPERT_TM) * EXPERT_TM
    n_tiles = p_rows // EXPERT_TM
    n_sub = p_rows // SUB
    n_visits = n_tiles + n_experts

    padded = jnp.maximum((counts + SUB - 1) // SUB, 1) * SUB
    e_ids = jnp.arange(n_experts, dtype=jnp.int32)
    pad_end = jnp.sum(jnp.where(e_ids[None, :] <= e_ids[:, None], padded[None, :], 0), axis=1)
    pad_start = pad_end - padded
    total = pad_end[-1]
    zstart = (pad_start + counts).astype(jnp.int32)
    tail = jnp.stack([total, (p_rows + SUB - total) // SUB]).astype(jnp.int32)

    sub = jnp.arange(n_sub, dtype=jnp.int32)
    pos = sub * SUB
    owner = jnp.sum((pad_end[None, :] <= pos[:, None]).astype(jnp.int32), axis=1)
    owner = jnp.where(pos < total, jnp.minimum(owner, n_experts - 1), n_experts)
    prev_owner = jnp.concatenate([jnp.full((1,), -1, jnp.int32), owner[:-1]])
    new_visit = ((sub % sub_per_tile == 0) | (owner != prev_owner)).astype(jnp.int32)
    visit = jnp.sum(jnp.where(sub[None, :] <= sub[:, None], new_visit[None, :], 0), axis=1) - 1
    bit = jnp.left_shift(1, sub % sub_per_tile).astype(jnp.int32)

    member = visit[None, :] == jnp.arange(n_visits, dtype=jnp.int32)[:, None]
    used = jnp.any(member, axis=1)
    vtile = jnp.where(used, jnp.max(jnp.where(member, sub // sub_per_tile, 0), axis=1), n_tiles - 1)
    vown = jnp.where(used, jnp.max(jnp.where(member, owner, 0), axis=1), n_experts)
    bits = jnp.sum(jnp.where(member, bit, 0), axis=1)
    first = jnp.any(member & (sub % sub_per_tile == 0)[None, :], axis=1).astype(jnp.int32)
    cmask = jnp.where(vown < n_experts, bits, 0)
    zmask = jnp.where(vown < n_experts, 0, bits)
    vexp = jnp.minimum(vown, n_experts - 1)
    return (pad_start.astype(jnp.int32), zstart, tail, vtile.astype(jnp.int32),
            vexp.astype(jnp.int32), cmask.astype(jnp.int32), zmask.astype(jnp.int32), first,
            p_rows, n_tiles)


def kernel(x, c, w_ada, b_ada, g_pre_mix, g_post_mix, w_in, conv_a_w, conv_b_w, conv_b_b, ln_b_g, ln_b_b, w_out, g_pre_ffn, g_post_ffn, w_router, b_router, w_gate, b_gate, w_up, b_up, w_down, b_down):
    bsz, seq, d = x.shape
    n = bsz * seq
    depth = w_ada.shape[0]
    n_experts = w_router.shape[-1]
    assert n_experts <= LANES
    xf = x.reshape(n, d)
    for l in range(depth):
        mod = _ada(c, w_ada[l], b_ada[l])
        n_mod = mod.shape[1] // d
        mod3 = mod.reshape(bsz, n_mod, d)

        wr = jnp.zeros((d, LANES), jnp.float32).at[:, :n_experts].set(w_router[l])
        wr_hi = wr.astype(jnp.bfloat16)
        wr_lo = (wr - wr_hi.astype(jnp.float32)).astype(jnp.bfloat16)
        br = jnp.full((1, LANES), NEG_BIG, jnp.float32).at[0, :n_experts].set(b_router[l])

        x1, hp, idx, wgt, rank, cnt = _mix(
            xf, mod3, seq, g_pre_mix[l].reshape(1, d), g_post_mix[l].reshape(1, d),
            g_pre_ffn[l].reshape(1, d), w_in[l].astype(jnp.bfloat16), conv_a_w[l], conv_b_w[l],
            conv_b_b[l].reshape(1, -1), ln_b_g[l].reshape(1, -1), ln_b_b[l].reshape(1, -1),
            w_out[l].astype(jnp.bfloat16), wr_hi, wr_lo, br)

        (pad_start, zstart, tail, vtile, vexp, cmask, zmask, first, p_rows,
         n_tiles) = _routing_tables(cnt[0, :n_experts], n * TOP_K)
        start_row = jnp.zeros((1, LANES), jnp.float32).at[0, :n_experts].set(
            pad_start.astype(jnp.float32))
        dest = _plan(idx, rank, start_row)
        dest_flat = dest[:, :TOP_K].reshape(n * TOP_K)

        slab_rows = d // (2 * LANES)
        xs = _dispatch(hp, dest_flat, zstart, tail, p_rows + SUB, slab_rows)
        ys = _expert(xs, w_gate[l], b_gate[l], w_up[l], b_up[l], w_down[l], b_down[l],
                     vtile, vexp, cmask, zmask, first, n_tiles)
        xf = _combine(ys, dest_flat, wgt, x1, mod3, seq, g_post_ffn[l].reshape(1, d))
    return xf.reshape(bsz, seq, d)
```

```python
import functools

import jax
import jax.numpy as jnp
from jax import lax
from jax.experimental import pallas as pl
from jax.experimental.pallas import tpu as pltpu

EPS = 1e-6
TOP_K = 4
CONV_A = 3
CONV_B = 31
SWIGLU_ALPHA = 1.702
SWIGLU_LIMIT = 7.0

LANES = 128
SUBLANES = 8
VMEM_LIMIT_BYTES = 56 * 1024 * 1024

ADA_TN = 1024
MIX_TM = 256
HALO_A = 8
HALO_B = 32
PLAN_TM = 2048
DISPATCH_TM = 1024
DISPATCH_UNROLL = 8
SUB = 256
EXPERT_ROWS = 2560
EXPERT_TF = 256
COMBINE_TM = 256
COMBINE_UNROLL = 8
NEG_BIG = -1e30


def _cparams(sem):
    return pltpu.CompilerParams(dimension_semantics=sem, vmem_limit_bytes=VMEM_LIMIT_BYTES)


def _rms(x, g):
    return x * lax.rsqrt(jnp.mean(x * x, axis=-1, keepdims=True) + EPS) * g


def _pack_slab_words(v):
    half = v.shape[1] // 2
    lo = pltpu.bitcast(v[:, :half].astype(jnp.bfloat16).astype(jnp.float32), jnp.uint32)
    hi = pltpu.bitcast(v[:, half:].astype(jnp.bfloat16).astype(jnp.float32), jnp.uint32)
    return (lo >> 16) | (hi & jnp.uint32(0xFFFF0000))


def _unpack_slab_words(w):
    lo = pltpu.bitcast(w << 16, jnp.float32)
    hi = pltpu.bitcast(w & jnp.uint32(0xFFFF0000), jnp.float32)
    return lo, hi


def _ada_kernel(c_ref, w_ref, b_ref, o_ref):
    c = c_ref[...]
    ca = (c * jax.nn.sigmoid(c)).astype(jnp.bfloat16)
    o_ref[...] = jnp.dot(ca, w_ref[...].astype(jnp.bfloat16),
                         preferred_element_type=jnp.float32) + b_ref[...]


def _ada(c, w_ada, b_ada):
    bsz, d = c.shape
    n_out = w_ada.shape[1]
    tn = min(ADA_TN, n_out)
    assert n_out % tn == 0 and bsz <= SUBLANES
    c8 = jnp.zeros((SUBLANES, d), jnp.float32).at[:bsz].set(c)
    out = pl.pallas_call(
        _ada_kernel,
        grid=(n_out // tn,),
        in_specs=[pl.BlockSpec((SUBLANES, d), lambda j: (0, 0)),
                  pl.BlockSpec((d, tn), lambda j: (0, j)),
                  pl.BlockSpec((1, tn), lambda j: (0, j))],
        out_specs=pl.BlockSpec((SUBLANES, tn), lambda j: (0, j)),
        out_shape=jax.ShapeDtypeStruct((SUBLANES, n_out), jnp.float32),
        compiler_params=_cparams(("arbitrary",)),
        name="ada",
    )(c8, w_ada, b_ada.reshape(1, n_out))
    return out[:bsz]


def _mix_kernel(x_ref, mod_ref, gpre_ref, gpost_ref, gffn_ref, win_ref, caw_ref, cbw_ref,
                cbb_ref, lng_ref, lnb_ref, wout_ref, wrh_ref, wrl_ref, br_ref,
                x1_ref, hp_ref, idx_ref, wgt_ref, rank_ref, cnt_ref,
                pa_buf, u_buf, v_buf, ycat, run_cnt, *, tiles_per_seq, wa, wb):
    i = pl.program_id(0)
    tm, d = x_ref.shape
    f32, bf16 = jnp.float32, jnp.bfloat16

    @pl.when(i == 0)
    def _():
        run_cnt[...] = jnp.zeros_like(run_cnt)

    @pl.when(i % tiles_per_seq == 0)
    def _():
        pa_buf[:, 0:HALO_A, :] = jnp.zeros((wa // LANES, HALO_A, LANES), f32)
        u_buf[:, 0:HALO_B, :] = jnp.zeros((wb // LANES, HALO_B, LANES), f32)

    x = x_ref[...]
    mod = mod_ref[0]
    h = (_rms(x, gpre_ref[...]) * (1.0 + mod[1:2]) + mod[0:1]).astype(bf16)

    proj_a = jnp.dot(h, win_ref[:, 0:3 * wa], preferred_element_type=f32)
    for cb in range(wa // LANES):
        sl = slice(cb * LANES, (cb + 1) * LANES)
        pa = proj_a[:, sl] * proj_a[:, 2 * wa + cb * LANES:2 * wa + (cb + 1) * LANES]
        pa_buf[cb, HALO_A:HALO_A + tm, :] = pa
        conv = caw_ref[CONV_A - 1:CONV_A, sl] * pa
        for k in range(CONV_A - 1):
            off = HALO_A - (CONV_A - 1) + k
            conv = conv + caw_ref[k:k + 1, sl] * pa_buf[cb, off:off + tm, :]
        ycat[:, sl] = (proj_a[:, wa + cb * LANES:wa + (cb + 1) * LANES] * conv).astype(bf16)
        pa_buf[cb, 0:HALO_A, :] = pa_buf[cb, tm:tm + HALO_A, :]

    proj_b = jnp.dot(h, win_ref[:, 3 * wa:3 * wa + 2 * wb], preferred_element_type=f32)
    for cb in range(wb // LANES):
        sl = slice(cb * LANES, (cb + 1) * LANES)
        u = proj_b[:, sl] * jax.nn.sigmoid(proj_b[:, wb + cb * LANES:wb + (cb + 1) * LANES])
        u_buf[cb, HALO_B:HALO_B + tm, :] = u
        acc = cbb_ref[:, sl] + cbw_ref[CONV_B - 1:CONV_B, sl] * u
        for k in range(CONV_B - 1):
            off = HALO_B - (CONV_B - 1) + k
            acc = acc + cbw_ref[k:k + 1, sl] * u_buf[cb, off:off + tm, :]
        v_buf[:, sl] = acc
        u_buf[cb, 0:HALO_B, :] = u_buf[cb, tm:tm + HALO_B, :]
    v = v_buf[...]
    mu = jnp.mean(v, axis=-1, keepdims=True)
    vc = v - mu
    var = jnp.mean(vc * vc, axis=-1, keepdims=True)
    yb = vc * lax.rsqrt(var + EPS) * lng_ref[...] + lnb_ref[...]
    ycat[:, wa:wa + wb] = (yb * jax.nn.sigmoid(yb)).astype(bf16)

    y = jnp.dot(ycat[...], wout_ref[...], preferred_element_type=f32)
    x1 = x + mod[2:3] * _rms(y, gpost_ref[...])
    x1_ref[...] = x1

    h2 = _rms(x1, gffn_ref[...]) * (1.0 + mod[4:5]) + mod[3:4]
    words = _pack_slab_words(h2)
    slab_rows = d // (2 * LANES)
    for j in range(slab_rows):
        hp_ref[pl.ds(j, tm, stride=slab_rows), :] = words[:, j * LANES:(j + 1) * LANES]

    h2_hi = h2.astype(bf16)
    h2_lo = (h2 - h2_hi.astype(f32)).astype(bf16)
    logits = (jnp.dot(h2_hi, wrh_ref[...], preferred_element_type=f32)
              + jnp.dot(h2_hi, wrl_ref[...], preferred_element_type=f32)
              + jnp.dot(h2_lo, wrh_ref[...], preferred_element_type=f32)
              + br_ref[...])

    lane = lax.broadcasted_iota(jnp.int32, (tm, LANES), 1)
    lane_f = lane.astype(f32)
    vals, idxs = [], []
    cur = logits
    for _ in range(TOP_K):
        m = jnp.max(cur, axis=-1, keepdims=True)
        ix = jnp.min(jnp.where(cur == m, lane_f, float(LANES)), axis=-1,
                     keepdims=True).astype(jnp.int32)
        vals.append(m)
        idxs.append(ix)
        cur = jnp.where(lane == ix, -jnp.inf, cur)
    exps = [jnp.exp(vk - vals[0]) for vk in vals]
    denom = exps[0]
    for ek in exps[1:]:
        denom = denom + ek
    inv = 1.0 / denom

    row = lax.broadcasted_iota(jnp.int32, (tm, tm), 0)
    col = lax.broadcasted_iota(jnp.int32, (tm, tm), 1)
    tri = jnp.where(col < row, 1.0, 0.0).astype(bf16)
    run = run_cnt[...]
    idx_out = jnp.zeros((tm, LANES), jnp.int32)
    wgt_out = jnp.zeros((tm, LANES), f32)
    rank_out = jnp.zeros((tm, LANES), f32)
    for k in range(TOP_K):
        oh = jnp.where(lane == idxs[k], 1.0, 0.0)
        before = jnp.dot(tri, oh.astype(bf16), preferred_element_type=f32)
        rank_k = jnp.sum(oh * (before + run), axis=-1, keepdims=True)
        run = run + jnp.sum(oh, axis=0, keepdims=True)
        idx_out = jnp.where(lane == k, idxs[k], idx_out)
        wgt_out = jnp.where(lane == k, exps[k] * inv, wgt_out)
        rank_out = jnp.where(lane == k, rank_k, rank_out)
    run_cnt[...] = run
    idx_ref[...] = idx_out
    wgt_ref[...] = wgt_out
    rank_ref[...] = rank_out.astype(jnp.int32)
    cnt_ref[...] = jnp.broadcast_to(run, cnt_ref.shape).astype(jnp.int32)


def _mix(x2, mod3, seq, g_pre, g_post, g_ffn, w_in, conv_a_w, conv_b_w, conv_b_b, ln_g, ln_b,
         w_out, wr_hi, wr_lo, br):
    n, d = x2.shape
    wa = conv_a_w.shape[1]
    wb = conv_b_w.shape[1]
    tm = MIX_TM
    assert seq % tm == 0 and tm >= HALO_B and wa % LANES == 0 and wb % LANES == 0
    assert wa + wb == d and d % (2 * LANES) == 0
    tiles_per_seq = seq // tm
    slab_rows = d // (2 * LANES)
    const = lambda shape: pl.BlockSpec(shape, lambda i: (0,) * len(shape))
    resident = lambda shape: pl.BlockSpec(shape, lambda i: (0,) * len(shape),
                                          pipeline_mode=pl.Buffered(1))
    row_block = lambda w: pl.BlockSpec((tm, w), lambda i: (i, 0))
    kernel = functools.partial(_mix_kernel, tiles_per_seq=tiles_per_seq, wa=wa, wb=wb)
    return pl.pallas_call(
        kernel,
        grid=(n // tm,),
        in_specs=[row_block(d),
                  pl.BlockSpec((1, mod3.shape[1], d), lambda i: (i // tiles_per_seq, 0, 0)),
                  const((1, d)), const((1, d)), const((1, d)),
                  resident(w_in.shape),
                  const(conv_a_w.shape), const(conv_b_w.shape),
                  const((1, wb)), const((1, wb)), const((1, wb)),
                  resident(w_out.shape),
                  const(wr_hi.shape), const(wr_lo.shape), const((1, LANES))],
        out_specs=[row_block(d),
                   pl.BlockSpec((tm * slab_rows, LANES), lambda i: (i, 0)),
                   row_block(LANES), row_block(LANES), row_block(LANES),
                   const((SUBLANES, LANES))],
        out_shape=[jax.ShapeDtypeStruct((n, d), jnp.float32),
                   jax.ShapeDtypeStruct((n * slab_rows, LANES), jnp.uint32),
                   jax.ShapeDtypeStruct((n, LANES), jnp.int32),
                   jax.ShapeDtypeStruct((n, LANES), jnp.float32),
                   jax.ShapeDtypeStruct((n, LANES), jnp.int32),
                   jax.ShapeDtypeStruct((SUBLANES, LANES), jnp.int32)],
        scratch_shapes=[pltpu.VMEM((wa // LANES, tm + HALO_A, LANES), jnp.float32),
                        pltpu.VMEM((wb // LANES, tm + HALO_B, LANES), jnp.float32),
                        pltpu.VMEM((tm, wb), jnp.float32),
                        pltpu.VMEM((tm, d), jnp.bfloat16),
                        pltpu.VMEM((1, LANES), jnp.float32)],
        compiler_params=_cparams(("arbitrary",)),
        name="mix",
    )(x2, mod3, g_pre, g_post, g_ffn, w_in, conv_a_w, conv_b_w, conv_b_b, ln_g, ln_b,
      w_out, wr_hi, wr_lo, br)


def _plan_kernel(idx_ref, rank_ref, start_ref, dest_ref):
    tm = idx_ref.shape[0]
    lane = lax.broadcasted_iota(jnp.int32, (tm, LANES), 1)
    idx = idx_ref[...]
    rank = rank_ref[...]
    start = start_ref[...]
    out = jnp.zeros((tm, LANES), jnp.int32)
    for k in range(TOP_K):
        base = jnp.sum(jnp.where(lane == idx[:, k:k + 1], start, 0.0), axis=-1, keepdims=True)
        out = jnp.where(lane == k, base.astype(jnp.int32) + rank[:, k:k + 1], out)
    dest_ref[...] = out


def _plan(idx, rank, start_row):
    n = idx.shape[0]
    tm = min(PLAN_TM, n)
    assert n % tm == 0
    blk = pl.BlockSpec((tm, LANES), lambda i: (i, 0))
    return pl.pallas_call(
        _plan_kernel,
        grid=(n // tm,),
        in_specs=[blk, blk, pl.BlockSpec((1, LANES), lambda i: (0, 0))],
        out_specs=blk,
        out_shape=jax.ShapeDtypeStruct((n, LANES), jnp.int32),
        compiler_params=_cparams(("arbitrary",)),
        name="plan",
    )(idx, rank, start_row)


def _slab(ref, row, slab_rows, count=1):
    return ref.at[pl.ds(pl.multiple_of(row * slab_rows, slab_rows), count * slab_rows)]


def _dispatch_kernel(zstart_ref, tail_ref, dest_ref, hp_ref, xs_ref, zbuf, sem, zsem,
                     *, tm, n_experts, slab_rows):
    i = pl.program_id(0)

    def zero_copy(start):
        return pltpu.make_async_copy(zbuf, _slab(xs_ref, start, slab_rows, SUB), zsem)

    @pl.when(i == 0)
    def _():
        zbuf[...] = jnp.zeros_like(zbuf)
        for parity in range(2):
            for e in range(parity, n_experts, 2):
                zero_copy(zstart_ref[e]).start()
            for e in range(parity, n_experts, 2):
                zero_copy(zstart_ref[e]).wait()

        def tail_start(t, carry):
            zero_copy(tail_ref[0] + t * SUB).start()
            return carry

        def tail_wait(t, carry):
            zero_copy(tail_ref[0] + t * SUB).wait()
            return carry
        lax.fori_loop(0, tail_ref[1], tail_start, 0)
        lax.fori_loop(0, tail_ref[1], tail_wait, 0)

    def body(t, carry):
        for u in range(DISPATCH_UNROLL):
            tok = t * DISPATCH_UNROLL + u
            src = _slab(hp_ref, tok, slab_rows)
            for k in range(TOP_K):
                dst = _slab(xs_ref, dest_ref[0, 0, tok * TOP_K + k], slab_rows)
                pltpu.make_async_copy(src, dst, sem).start(priority=k % 2)
        return carry
    lax.fori_loop(0, tm // DISPATCH_UNROLL, body, 0)
    for k in range(TOP_K):
        pltpu.make_async_copy(hp_ref, _slab(xs_ref, 0, slab_rows, tm), sem).wait()


def _dispatch(hp, dest_flat, zstart, tail, p_rows, slab_rows):
    n = hp.shape[0] // slab_rows
    tm = min(DISPATCH_TM, n)
    assert n % tm == 0 and tm % DISPATCH_UNROLL == 0
    n_experts = zstart.shape[0]
    dest3 = dest_flat.reshape(n // tm, 1, tm * TOP_K)
    kernel = functools.partial(_dispatch_kernel, tm=tm, n_experts=n_experts, slab_rows=slab_rows)
    return pl.pallas_call(
        kernel,
        grid_spec=pltpu.PrefetchScalarGridSpec(
            num_scalar_prefetch=2,
            grid=(n // tm,),
            in_specs=[pl.BlockSpec((1, 1, tm * TOP_K), lambda i, *_: (i, 0, 0),
                                   memory_space=pltpu.SMEM),
                      pl.BlockSpec((tm * slab_rows, LANES), lambda i, *_: (i, 0))],
            out_specs=pl.BlockSpec(memory_space=pl.ANY),
            scratch_shapes=[pltpu.VMEM((SUB * slab_rows, LANES), jnp.uint32),
                            pltpu.SemaphoreType.DMA(()),
                            pltpu.SemaphoreType.DMA(())]),
        out_shape=jax.ShapeDtypeStruct((p_rows * slab_rows, LANES), jnp.uint32),
        compiler_params=_cparams(("arbitrary",)),
        name="dispatch",
    )(zstart, tail, dest3, hp)


def _expert_kernel(vexp_ref, vrow_ref, vnsub_ref, tail_ref,
                   xs_ref, wg_ref, wu_ref, wd_ref, bg_ref, bu_ref, bd_ref, ys_ref,
                   xb, acc, wgb, wub, wdb, xstage, ostage, xsem, osem, zsem, *, n_f):
    del vexp_ref
    v = pl.program_id(0)
    j = pl.program_id(1)
    d = xb.shape[1]
    slab_rows = d // (2 * LANES)
    stage_rows = SUB * slab_rows
    f32, bf16 = jnp.float32, jnp.bfloat16
    nsub = vnsub_ref[v]
    row0 = vrow_ref[v]

    def stage_slot(ref, slot):
        return ref.at[pl.ds(pl.multiple_of(slot * stage_rows, stage_rows), stage_rows)]

    def x_copy(s, slot):
        return pltpu.make_async_copy(_slab(xs_ref, row0 + s * SUB, slab_rows, SUB),
                                     stage_slot(xstage, slot), xsem.at[slot])

    def o_copy(s, slot):
        return pltpu.make_async_copy(stage_slot(ostage, slot),
                                     _slab(ys_ref, row0 + s * SUB, slab_rows, SUB), osem.at[slot])

    @pl.when((v == 0) & (j == 0))
    def _():
        ostage[0:stage_rows, :] = jnp.zeros((stage_rows, LANES), jnp.uint32)

        def z_copy(t):
            return pltpu.make_async_copy(stage_slot(ostage, 0),
                                         _slab(ys_ref, tail_ref[0] + t * SUB, slab_rows, SUB), zsem)

        def z_start(t, carry):
            z_copy(t).start()
            return carry

        def z_wait(t, carry):
            z_copy(t).wait()
            return carry
        lax.fori_loop(0, tail_ref[1], z_start, 0)
        lax.fori_loop(0, tail_ref[1], z_wait, 0)

    @pl.when((j == 0) & (nsub > 0))
    def _():
        x_copy(0, 0).start()

        def body(s, carry):
            slot = s % 2

            @pl.when(s + 1 < nsub)
            def _():
                x_copy(s + 1, 1 - slot).start()
            x_copy(s, slot).wait()
            base = pl.multiple_of(s * SUB, SUB)
            for jj in range(slab_rows):
                lo, hi = _unpack_slab_words(
                    xstage[pl.ds(slot * stage_rows + jj, SUB, stride=slab_rows), :])
                xb[pl.ds(base, SUB), jj * LANES:(jj + 1) * LANES] = lo.astype(bf16)
                xb[pl.ds(base, SUB), d // 2 + jj * LANES:d // 2 + (jj + 1) * LANES] = hi.astype(bf16)
            acc[pl.ds(base, SUB), :] = jnp.zeros((SUB, d), f32)
            return carry
        lax.fori_loop(0, nsub, body, 0)

    @pl.when(nsub > 0)
    def _():
        wgb[...] = wg_ref[...].astype(bf16)
        wub[...] = wu_ref[...].astype(bf16)
        wdb[...] = wd_ref[...].astype(bf16)

        def chain(base, m):
            rows = pl.ds(base, m)
            xt = xb[rows, :]
            g = jnp.dot(xt, wgb[...], preferred_element_type=f32) + bg_ref[...]
            u = jnp.dot(xt, wub[...], preferred_element_type=f32) + bu_ref[...]
            g = jnp.minimum(g, SWIGLU_LIMIT)
            u = jnp.clip(u, -SWIGLU_LIMIT, SWIGLU_LIMIT)
            a = (u + 1.0) * (g * jax.nn.sigmoid(SWIGLU_ALPHA * g))
            acc[rows, :] += jnp.dot(a.astype(bf16), wdb[...], preferred_element_type=f32)

        def pair(p, carry):
            chain(pl.multiple_of(p * (2 * SUB), 2 * SUB), 2 * SUB)
            return carry
        lax.fori_loop(0, nsub // 2, pair, 0)

        @pl.when(nsub % 2 == 1)
        def _():
            chain(pl.multiple_of((nsub - 1) * SUB, SUB), SUB)

    @pl.when((j == n_f - 1) & (nsub > 0))
    def _():
        def body(s, carry):
            slot = s % 2

            @pl.when(s >= 2)
            def _():
                o_copy(s - 2, slot).wait()
            words = _pack_slab_words(acc[pl.ds(pl.multiple_of(s * SUB, SUB), SUB), :] + bd_ref[...])
            for jj in range(slab_rows):
                ostage[pl.ds(slot * stage_rows + jj, SUB, stride=slab_rows), :] = (
                    words[:, jj * LANES:(jj + 1) * LANES])
            o_copy(s, slot).start()
            return carry
        lax.fori_loop(0, nsub, body, 0)

        @pl.when(nsub >= 2)
        def _():
            o_copy(nsub - 2, nsub % 2).wait()
        o_copy(nsub - 1, (nsub - 1) % 2).wait()


def _expert(xs, w_gate, b_gate, w_up, b_up, w_down, b_down, vexp, vrow, vnsub, tail, p_rows):
    n_experts, d, f = w_gate.shape
    tf = min(EXPERT_TF, f)
    assert f % tf == 0 and EXPERT_ROWS % (2 * SUB) == 0
    n_f = f // tf
    slab_rows = d // (2 * LANES)
    n_visits = vexp.shape[0]

    def w_idx(v, j, ve, vr, vn, tl):
        return jnp.where(vn[v] == 0, n_f - 1, j)

    kernel = functools.partial(_expert_kernel, n_f=n_f)
    return pl.pallas_call(
        kernel,
        grid_spec=pltpu.PrefetchScalarGridSpec(
            num_scalar_prefetch=4,
            grid=(n_visits, n_f),
            in_specs=[
                pl.BlockSpec(memory_space=pl.ANY),
                pl.BlockSpec((None, d, tf), lambda v, j, ve, *s: (ve[v], 0, w_idx(v, j, ve, *s))),
                pl.BlockSpec((None, d, tf), lambda v, j, ve, *s: (ve[v], 0, w_idx(v, j, ve, *s))),
                pl.BlockSpec((None, tf, d), lambda v, j, ve, *s: (ve[v], w_idx(v, j, ve, *s), 0)),
                pl.BlockSpec((None, 1, tf), lambda v, j, ve, *s: (ve[v], 0, w_idx(v, j, ve, *s))),
                pl.BlockSpec((None, 1, tf), lambda v, j, ve, *s: (ve[v], 0, w_idx(v, j, ve, *s))),
                pl.BlockSpec((None, 1, d), lambda v, j, ve, *s: (ve[v], 0, 0)),
            ],
            out_specs=pl.BlockSpec(memory_space=pl.ANY),
            scratch_shapes=[pltpu.VMEM((EXPERT_ROWS, d), jnp.bfloat16),
                            pltpu.VMEM((EXPERT_ROWS, d), jnp.float32),
                            pltpu.VMEM((d, tf), jnp.bfloat16),
                            pltpu.VMEM((d, tf), jnp.bfloat16),
                            pltpu.VMEM((tf, d), jnp.bfloat16),
                            pltpu.VMEM((2 * SUB * slab_rows, LANES), jnp.uint32),
                            pltpu.VMEM((2 * SUB * slab_rows, LANES), jnp.uint32),
                            pltpu.SemaphoreType.DMA((2,)),
                            pltpu.SemaphoreType.DMA((2,)),
                            pltpu.SemaphoreType.DMA(())]),
        out_shape=jax.ShapeDtypeStruct((p_rows * slab_rows, LANES), jnp.uint32),
        compiler_params=_cparams(("arbitrary", "arbitrary")),
        name="expert",
    )(vexp, vrow, vnsub, tail, xs, w_gate, w_up, w_down,
      b_gate.reshape(n_experts, 1, f), b_up.reshape(n_experts, 1, f),
      b_down.reshape(n_experts, 1, d))


def _combine_kernel(dest_ref, dnext_ref, ys_ref, wgt_ref, x1_ref, mod_ref, g_ref, o_ref,
                    buf, ybuf, sem, *, tm):
    i = pl.program_id(0)
    d = x1_ref.shape[1]
    slab_rows = d // (2 * LANES)
    half = TOP_K * tm

    def gather(idx_ref, slot):
        def body(t, carry):
            for u in range(COMBINE_UNROLL):
                tok = t * COMBINE_UNROLL + u
                for k in range(TOP_K):
                    src = _slab(ys_ref, idx_ref[0, 0, tok * TOP_K + k], slab_rows)
                    dst = _slab(buf, slot * half + k * tm + tok, slab_rows)
                    pltpu.make_async_copy(src, dst, sem.at[slot]).start(priority=k % 2)
            return carry
        lax.fori_loop(0, tm // COMBINE_UNROLL, body, 0)

    slot = i % 2

    @pl.when(i == 0)
    def _():
        gather(dest_ref, 0)

    @pl.when(i + 1 < pl.num_programs(0))
    def _():
        gather(dnext_ref, 1 - slot)

    pltpu.make_async_copy(_slab(ys_ref, 0, slab_rows, half), _slab(buf, slot * half, slab_rows, half),
                          sem.at[slot]).wait()

    wgt = wgt_ref[...]
    base = slot * half * slab_rows
    for jj in range(slab_rows):
        ylo = jnp.zeros((tm, LANES), jnp.float32)
        yhi = jnp.zeros((tm, LANES), jnp.float32)
        for k in range(TOP_K):
            lo, hi = _unpack_slab_words(
                buf[pl.ds(base + k * tm * slab_rows + jj, tm, stride=slab_rows), :])
            ylo = ylo + wgt[:, k:k + 1] * lo
            yhi = yhi + wgt[:, k:k + 1] * hi
        ybuf[:, jj * LANES:(jj + 1) * LANES] = ylo
        ybuf[:, d // 2 + jj * LANES:d // 2 + (jj + 1) * LANES] = yhi
    mod = mod_ref[0]
    o_ref[...] = x1_ref[...] + mod[5:6] * _rms(ybuf[...], g_ref[...])


def _combine(ys, dest_flat, wgt, x1, mod3, seq, g_post):
    n, d = x1.shape
    slab_rows = d // (2 * LANES)
    tm = COMBINE_TM
    assert seq % tm == 0
    tiles_per_seq = seq // tm
    n_tiles = n // tm
    dest3 = dest_flat.reshape(n_tiles, 1, tm * TOP_K)
    kernel = functools.partial(_combine_kernel, tm=tm)
    return pl.pallas_call(
        kernel,
        grid=(n_tiles,),
        in_specs=[pl.BlockSpec((1, 1, tm * TOP_K), lambda i: (i, 0, 0), memory_space=pltpu.SMEM),
                  pl.BlockSpec((1, 1, tm * TOP_K), lambda i: (jnp.minimum(i + 1, n_tiles - 1), 0, 0),
                               memory_space=pltpu.SMEM),
                  pl.BlockSpec(memory_space=pl.ANY),
                  pl.BlockSpec((tm, LANES), lambda i: (i, 0)),
                  pl.BlockSpec((tm, d), lambda i: (i, 0)),
                  pl.BlockSpec((1, mod3.shape[1], d), lambda i: (i // tiles_per_seq, 0, 0)),
                  pl.BlockSpec((1, d), lambda i: (0, 0))],
        out_specs=pl.BlockSpec((tm, d), lambda i: (i, 0)),
        out_shape=jax.ShapeDtypeStruct((n, d), jnp.float32),
        scratch_shapes=[pltpu.VMEM((2 * TOP_K * tm * slab_rows, LANES), jnp.uint32),
                        pltpu.VMEM((tm, d), jnp.float32),
                        pltpu.SemaphoreType.DMA((2,))],
        compiler_params=_cparams(("arbitrary",)),
        name="combine",
    )(dest3, dest3, ys, wgt, x1, mod3, g_post)


def _routing_tables(counts, n_assign):
    n_experts = counts.shape[0]
    p_rows = n_assign + n_experts * SUB
    n_visits = n_experts + -(-p_rows // EXPERT_ROWS)

    padded = jnp.maximum((counts + SUB - 1) // SUB, 1) * SUB
    e_ids = jnp.arange(n_experts, dtype=jnp.int32)
    lower = e_ids[None, :] <= e_ids[:, None]
    pad_end = jnp.sum(jnp.where(lower, padded[None, :], 0), axis=1)
    pad_start = pad_end - padded
    total = pad_end[-1]
    zstart = (pad_start + counts).astype(jnp.int32)
    tail_xs = jnp.stack([total, (p_rows + SUB - total) // SUB]).astype(jnp.int32)
    tail_ys = jnp.stack([total, (p_rows - total) // SUB]).astype(jnp.int32)

    n_chunk = (padded + EXPERT_ROWS - 1) // EXPERT_ROWS
    chunk_end = jnp.sum(jnp.where(lower, n_chunk[None, :], 0), axis=1)
    visit = jnp.arange(n_visits, dtype=jnp.int32)
    used = visit < chunk_end[-1]
    vexp = jnp.minimum(jnp.sum((chunk_end[None, :] <= visit[:, None]).astype(jnp.int32), axis=1),
                       n_experts - 1)
    onehot = vexp[:, None] == e_ids[None, :]
    pick = lambda a: jnp.sum(jnp.where(onehot, a[None, :], 0), axis=1)
    chunk = visit - pick(chunk_end - n_chunk)
    vrow = jnp.where(used, pick(pad_start) + chunk * EXPERT_ROWS, 0)
    vnsub = jnp.where(used, jnp.minimum(pick(padded) - chunk * EXPERT_ROWS, EXPERT_ROWS) // SUB, 0)
    return (pad_start.astype(jnp.int32), zstart, tail_xs, tail_ys, vexp.astype(jnp.int32),
            vrow.astype(jnp.int32), vnsub.astype(jnp.int32), p_rows)


def kernel(x, c, w_ada, b_ada, g_pre_mix, g_post_mix, w_in, conv_a_w, conv_b_w, conv_b_b, ln_b_g, ln_b_b, w_out, g_pre_ffn, g_post_ffn, w_router, b_router, w_gate, b_gate, w_up, b_up, w_down, b_down):
    bsz, seq, d = x.shape
    n = bsz * seq
    depth = w_ada.shape[0]
    n_experts = w_router.shape[-1]
    assert n_experts <= LANES
    xf = x.reshape(n, d)
    for l in range(depth):
        mod = _ada(c, w_ada[l], b_ada[l])
        n_mod = mod.shape[1] // d
        mod3 = mod.reshape(bsz, n_mod, d)

        wr = jnp.zeros((d, LANES), jnp.float32).at[:, :n_experts].set(w_router[l])
        wr_hi = wr.astype(jnp.bfloat16)
        wr_lo = (wr - wr_hi.astype(jnp.float32)).astype(jnp.bfloat16)
        br = jnp.full((1, LANES), NEG_BIG, jnp.float32).at[0, :n_experts].set(b_router[l])

        x1, hp, idx, wgt, rank, cnt = _mix(
            xf, mod3, seq, g_pre_mix[l].reshape(1, d), g_post_mix[l].reshape(1, d),
            g_pre_ffn[l].reshape(1, d), w_in[l].astype(jnp.bfloat16), conv_a_w[l], conv_b_w[l],
            conv_b_b[l].reshape(1, -1), ln_b_g[l].reshape(1, -1), ln_b_b[l].reshape(1, -1),
            w_out[l].astype(jnp.bfloat16), wr_hi, wr_lo, br)

        (pad_start, zstart, tail_xs, tail_ys, vexp, vrow, vnsub,
         p_rows) = _routing_tables(cnt[0, :n_experts], n * TOP_K)
        start_row = jnp.zeros((1, LANES), jnp.float32).at[0, :n_experts].set(
            pad_start.astype(jnp.float32))
        dest = _plan(idx, rank, start_row)
        dest_flat = dest[:, :TOP_K].reshape(n * TOP_K)

        slab_rows = d // (2 * LANES)
        xs = _dispatch(hp, dest_flat, zstart, tail_xs, p_rows + SUB, slab_rows)
        ys = _expert(xs, w_gate[l], b_gate[l], w_up[l], b_up[l], w_down[l], b_down[l],
                     vexp, vrow, vnsub, tail_ys, p_rows)
        xf = _combine(ys, dest_flat, wgt, x1, mod3, seq, g_post_ffn[l].reshape(1, d))
    return xf.reshape(bsz, seq, d)
```

```python
import functools

import jax
import jax.numpy as jnp
from jax import lax
from jax.experimental import pallas as pl
from jax.experimental.pallas import tpu as pltpu

EPS = 1e-6
TOP_K = 4
CONV_A = 3
CONV_B = 31
SWIGLU_ALPHA = 1.702
SWIGLU_LIMIT = 7.0

LANES = 128
SUBLANES = 8
VMEM_LIMIT_BYTES = 56 * 1024 * 1024

ADA_TN = 1024
MIX_TM = 256
HALO_A = 8
HALO_B = 32
PLAN_TM = 2048
DISPATCH_TM = 1024
DISPATCH_UNROLL = 8
SUB = 256
EXPERT_ROWS = 2560
EXPERT_TF = 256
X_AHEAD = 3
X_SLOTS = X_AHEAD + 1
COMBINE_TM = 256
COMBINE_UNROLL = 8
NEG_BIG = -1e30


def _cparams(sem):
    return pltpu.CompilerParams(dimension_semantics=sem, vmem_limit_bytes=VMEM_LIMIT_BYTES)


def _rms(x, g):
    return x * lax.rsqrt(jnp.mean(x * x, axis=-1, keepdims=True) + EPS) * g


def _pack_slab_words(v):
    half = v.shape[1] // 2
    lo = pltpu.bitcast(v[:, :half].astype(jnp.bfloat16).astype(jnp.float32), jnp.uint32)
    hi = pltpu.bitcast(v[:, half:].astype(jnp.bfloat16).astype(jnp.float32), jnp.uint32)
    return (lo >> 16) | (hi & jnp.uint32(0xFFFF0000))


def _unpack_slab_words(w):
    lo = pltpu.bitcast(w << 16, jnp.float32)
    hi = pltpu.bitcast(w & jnp.uint32(0xFFFF0000), jnp.float32)
    return lo, hi


def _ada_kernel(c_ref, w_ref, b_ref, o_ref):
    c = c_ref[...]
    ca = (c * jax.nn.sigmoid(c)).astype(jnp.bfloat16)
    o_ref[...] = jnp.dot(ca, w_ref[...].astype(jnp.bfloat16),
                         preferred_element_type=jnp.float32) + b_ref[...]


def _ada(c, w_ada, b_ada):
    bsz, d = c.shape
    n_out = w_ada.shape[1]
    tn = min(ADA_TN, n_out)
    assert n_out % tn == 0 and bsz <= SUBLANES
    c8 = jnp.zeros((SUBLANES, d), jnp.float32).at[:bsz].set(c)
    out = pl.pallas_call(
        _ada_kernel,
        grid=(n_out // tn,),
        in_specs=[pl.BlockSpec((SUBLANES, d), lambda j: (0, 0)),
                  pl.BlockSpec((d, tn), lambda j: (0, j)),
                  pl.BlockSpec((1, tn), lambda j: (0, j))],
        out_specs=pl.BlockSpec((SUBLANES, tn), lambda j: (0, j)),
        out_shape=jax.ShapeDtypeStruct((SUBLANES, n_out), jnp.float32),
        compiler_params=_cparams(("arbitrary",)),
        name="ada",
    )(c8, w_ada, b_ada.reshape(1, n_out))
    return out[:bsz]


def _mix_kernel(x_ref, xp_ref, mod_ref, modp_ref, gpre_ref, gpost_ref, gffn_ref, win_ref,
                caw_ref, cbw_ref, cbb_ref, lng_ref, lnb_ref, wout_ref, wrc_ref, wrh_ref, br_ref,
                x1_ref, hp_ref, idx_ref, wgt_ref, rank_ref, cnt_ref,
                pa_buf, u_buf, v_buf, ycat, run_cnt, *, tiles_per_seq, wa, wb):
    i = pl.program_id(0)
    tm, d = x_ref.shape
    f32, bf16 = jnp.float32, jnp.bfloat16

    @pl.when(i == 0)
    def _():
        run_cnt[...] = jnp.zeros_like(run_cnt)
        ycat[...] = jnp.zeros_like(ycat)

    @pl.when(i % tiles_per_seq == 0)
    def _():
        pa_buf[:, 0:HALO_A, :] = jnp.zeros((wa // LANES, HALO_A, LANES), f32)
        u_buf[:, 0:HALO_B, :] = jnp.zeros((wb // LANES, HALO_B, LANES), f32)

    y_prev = jnp.dot(ycat[...], wout_ref[...], preferred_element_type=f32)

    x = x_ref[...]
    mod = mod_ref[0]
    h = (_rms(x, gpre_ref[...]) * (1.0 + mod[1:2]) + mod[0:1]).astype(bf16)
    proj_b = jnp.dot(h, win_ref[:, 3 * wa:3 * wa + 2 * wb], preferred_element_type=f32)
    proj_a = jnp.dot(h, win_ref[:, 0:3 * wa], preferred_element_type=f32)

    for cb in range(wa // LANES):
        sl = slice(cb * LANES, (cb + 1) * LANES)
        pa = proj_a[:, sl] * proj_a[:, 2 * wa + cb * LANES:2 * wa + (cb + 1) * LANES]
        pa_buf[cb, HALO_A:HALO_A + tm, :] = pa
        conv = caw_ref[CONV_A - 1:CONV_A, sl] * pa
        for k in range(CONV_A - 1):
            off = HALO_A - (CONV_A - 1) + k
            conv = conv + caw_ref[k:k + 1, sl] * pa_buf[cb, off:off + tm, :]
        ycat[:, sl] = (proj_a[:, wa + cb * LANES:wa + (cb + 1) * LANES] * conv).astype(bf16)
        pa_buf[cb, 0:HALO_A, :] = pa_buf[cb, tm:tm + HALO_A, :]

    for cb in range(wb // LANES):
        sl = slice(cb * LANES, (cb + 1) * LANES)
        u = proj_b[:, sl] * jax.nn.sigmoid(proj_b[:, wb + cb * LANES:wb + (cb + 1) * LANES])
        u_buf[cb, HALO_B:HALO_B + tm, :] = u
        acc = cbb_ref[:, sl] + cbw_ref[CONV_B - 1:CONV_B, sl] * u
        for k in range(CONV_B - 1):
            off = HALO_B - (CONV_B - 1) + k
            acc = acc + cbw_ref[k:k + 1, sl] * u_buf[cb, off:off + tm, :]
        v_buf[:, sl] = acc
        u_buf[cb, 0:HALO_B, :] = u_buf[cb, tm:tm + HALO_B, :]
    v = v_buf[...]
    mu = jnp.mean(v, axis=-1, keepdims=True)
    vc = v - mu
    var = jnp.mean(vc * vc, axis=-1, keepdims=True)
    yb = vc * lax.rsqrt(var + EPS) * lng_ref[...] + lnb_ref[...]
    ycat[:, wa:wa + wb] = (yb * jax.nn.sigmoid(yb)).astype(bf16)

    _mix_stage2(i >= 1, y_prev, xp_ref, modp_ref, gpost_ref, gffn_ref, wrc_ref, wrh_ref, br_ref,
                x1_ref, hp_ref, idx_ref, wgt_ref, rank_ref, cnt_ref, run_cnt)


def _mix_stage2(valid, y, x_ref, mod_ref, gpost_ref, gffn_ref, wrc_ref, wrh_ref, br_ref,
                x1_ref, hp_ref, idx_ref, wgt_ref, rank_ref, cnt_ref, run_cnt):
    tm, d = x_ref.shape
    f32, bf16 = jnp.float32, jnp.bfloat16
    x = x_ref[...]
    mod = mod_ref[0]
    x1 = x + mod[2:3] * _rms(y, gpost_ref[...])
    x1_ref[...] = x1

    h2 = _rms(x1, gffn_ref[...]) * (1.0 + mod[4:5]) + mod[3:4]
    words = _pack_slab_words(h2)
    slab_rows = d // (2 * LANES)
    for j in range(slab_rows):
        hp_ref[pl.ds(j, tm, stride=slab_rows), :] = words[:, j * LANES:(j + 1) * LANES]

    h2_hi = h2.astype(bf16)
    h2_lo = (h2 - h2_hi.astype(f32)).astype(bf16)
    both = jnp.dot(h2_hi, wrc_ref[...], preferred_element_type=f32)
    logits = (both[:, :LANES] + both[:, LANES:]
              + jnp.dot(h2_lo, wrh_ref[...], preferred_element_type=f32)
              + br_ref[...])

    lane = lax.broadcasted_iota(jnp.int32, (tm, LANES), 1)
    lane_f = lane.astype(f32)
    vals, idxs = [], []
    cur = logits
    for _ in range(TOP_K):
        m = jnp.max(cur, axis=-1, keepdims=True)
        ix = jnp.min(jnp.where(cur == m, lane_f, float(LANES)), axis=-1,
                     keepdims=True).astype(jnp.int32)
        vals.append(m)
        idxs.append(ix)
        cur = jnp.where(lane == ix, -jnp.inf, cur)
    exps = [jnp.exp(vk - vals[0]) for vk in vals]
    denom = exps[0]
    for ek in exps[1:]:
        denom = denom + ek
    inv = 1.0 / denom

    row = lax.broadcasted_iota(jnp.int32, (tm, tm), 0)
    col = lax.broadcasted_iota(jnp.int32, (tm, tm), 1)
    tri = jnp.where(col < row, 1.0, 0.0).astype(bf16)
    run = run_cnt[...]
    idx_out = jnp.zeros((tm, LANES), jnp.int32)
    wgt_out = jnp.zeros((tm, LANES), f32)
    rank_out = jnp.zeros((tm, LANES), f32)
    for k in range(TOP_K):
        oh = jnp.where(lane == idxs[k], 1.0, 0.0)
        before = jnp.dot(tri, oh.astype(bf16), preferred_element_type=f32)
        rank_k = jnp.sum(oh * (before + run), axis=-1, keepdims=True)
        run = run + jnp.sum(oh, axis=0, keepdims=True)
        idx_out = jnp.where(lane == k, idxs[k], idx_out)
        wgt_out = jnp.where(lane == k, exps[k] * inv, wgt_out)
        rank_out = jnp.where(lane == k, rank_k, rank_out)
    run = jnp.where(valid, run, run_cnt[...])
    run_cnt[...] = run
    idx_ref[...] = idx_out
    wgt_ref[...] = wgt_out
    rank_ref[...] = rank_out.astype(jnp.int32)
    cnt_ref[...] = jnp.broadcast_to(run, cnt_ref.shape).astype(jnp.int32)


def _mix(x2, mod3, seq, g_pre, g_post, g_ffn, w_in, conv_a_w, conv_b_w, conv_b_b, ln_g, ln_b,
         w_out, wr_cat, wr_hi, br):
    n, d = x2.shape
    wa = conv_a_w.shape[1]
    wb = conv_b_w.shape[1]
    tm = MIX_TM
    assert seq % tm == 0 and tm >= HALO_B and wa % LANES == 0 and wb % LANES == 0
    assert wa + wb == d and d % (2 * LANES) == 0
    tiles_per_seq = seq // tm
    n_tiles = n // tm
    slab_rows = d // (2 * LANES)
    cur = lambda i: jnp.minimum(i, n_tiles - 1)
    prev = lambda i: jnp.maximum(i - 1, 0)
    const = lambda shape: pl.BlockSpec(shape, lambda i: (0,) * len(shape))
    resident = lambda shape: pl.BlockSpec(shape, lambda i: (0,) * len(shape),
                                          pipeline_mode=pl.Buffered(1))
    out_block = lambda w: pl.BlockSpec((tm, w), lambda i: (prev(i), 0))
    mod_block = lambda tile: pl.BlockSpec((1, mod3.shape[1], d),
                                          lambda i: (tile(i) // tiles_per_seq, 0, 0))
    kernel = functools.partial(_mix_kernel, tiles_per_seq=tiles_per_seq, wa=wa, wb=wb)
    return pl.pallas_call(
        kernel,
        grid=(n_tiles + 1,),
        in_specs=[pl.BlockSpec((tm, d), lambda i: (cur(i), 0)),
                  pl.BlockSpec((tm, d), lambda i: (prev(i), 0)),
                  mod_block(cur), mod_block(prev),
                  const((1, d)), const((1, d)), const((1, d)),
                  resident(w_in.shape),
                  const(conv_a_w.shape), const(conv_b_w.shape),
                  const((1, wb)), const((1, wb)), const((1, wb)),
                  resident(w_out.shape),
                  const(wr_cat.shape), const(wr_hi.shape), const((1, LANES))],
        out_specs=[out_block(d),
                   pl.BlockSpec((tm * slab_rows, LANES), lambda i: (prev(i), 0)),
                   out_block(LANES), out_block(LANES), out_block(LANES),
                   const((SUBLANES, LANES))],
        out_shape=[jax.ShapeDtypeStruct((n, d), jnp.float32),
                   jax.ShapeDtypeStruct((n * slab_rows, LANES), jnp.uint32),
                   jax.ShapeDtypeStruct((n, LANES), jnp.int32),
                   jax.ShapeDtypeStruct((n, LANES), jnp.float32),
                   jax.ShapeDtypeStruct((n, LANES), jnp.int32),
                   jax.ShapeDtypeStruct((SUBLANES, LANES), jnp.int32)],
        scratch_shapes=[pltpu.VMEM((wa // LANES, tm + HALO_A, LANES), jnp.float32),
                        pltpu.VMEM((wb // LANES, tm + HALO_B, LANES), jnp.float32),
                        pltpu.VMEM((tm, wb), jnp.float32),
                        pltpu.VMEM((tm, d), jnp.bfloat16),
                        pltpu.VMEM((1, LANES), jnp.float32)],
        compiler_params=_cparams(("arbitrary",)),
        name="mix",
    )(x2, x2, mod3, mod3, g_pre, g_post, g_ffn, w_in, conv_a_w, conv_b_w, conv_b_b, ln_g, ln_b,
      w_out, wr_cat, wr_hi, br)


def _plan_kernel(idx_ref, rank_ref, start_ref, dest_ref):
    tm = idx_ref.shape[0]
    lane = lax.broadcasted_iota(jnp.int32, (tm, LANES), 1)
    idx = idx_ref[...]
    rank = rank_ref[...]
    start = start_ref[...]
    out = jnp.zeros((tm, LANES), jnp.int32)
    for k in range(TOP_K):
        base = jnp.sum(jnp.where(lane == idx[:, k:k + 1], start, 0.0), axis=-1, keepdims=True)
        out = jnp.where(lane == k, base.astype(jnp.int32) + rank[:, k:k + 1], out)
    dest_ref[...] = out


def _plan(idx, rank, start_row):
    n = idx.shape[0]
    tm = min(PLAN_TM, n)
    assert n % tm == 0
    blk = pl.BlockSpec((tm, LANES), lambda i: (i, 0))
    return pl.pallas_call(
        _plan_kernel,
        grid=(n // tm,),
        in_specs=[blk, blk, pl.BlockSpec((1, LANES), lambda i: (0, 0))],
        out_specs=blk,
        out_shape=jax.ShapeDtypeStruct((n, LANES), jnp.int32),
        compiler_params=_cparams(("arbitrary",)),
        name="plan",
    )(idx, rank, start_row)


def _slab(ref, row, slab_rows, count=1):
    return ref.at[pl.ds(pl.multiple_of(row * slab_rows, slab_rows), count * slab_rows)]


def _dispatch_kernel(zstart_ref, tail_ref, dest_ref, hp_ref, xs_ref, zbuf, sem, zsem,
                     *, tm, n_experts, slab_rows):
    i = pl.program_id(0)

    def zero_copy(start):
        return pltpu.make_async_copy(zbuf, _slab(xs_ref, start, slab_rows, SUB), zsem)

    @pl.when(i == 0)
    def _():
        zbuf[...] = jnp.zeros_like(zbuf)
        for parity in range(2):
            for e in range(parity, n_experts, 2):
                zero_copy(zstart_ref[e]).start()
            for e in range(parity, n_experts, 2):
                zero_copy(zstart_ref[e]).wait()

        def tail_start(t, carry):
            zero_copy(tail_ref[0] + t * SUB).start()
            return carry

        def tail_wait(t, carry):
            zero_copy(tail_ref[0] + t * SUB).wait()
            return carry
        lax.fori_loop(0, tail_ref[1], tail_start, 0)
        lax.fori_loop(0, tail_ref[1], tail_wait, 0)

    def body(t, carry):
        for u in range(DISPATCH_UNROLL):
            tok = t * DISPATCH_UNROLL + u
            src = _slab(hp_ref, tok, slab_rows)
            for k in range(TOP_K):
                dst = _slab(xs_ref, dest_ref[0, 0, tok * TOP_K + k], slab_rows)
                pltpu.make_async_copy(src, dst, sem).start(priority=k % 2)
        return carry
    lax.fori_loop(0, tm // DISPATCH_UNROLL, body, 0)
    for k in range(TOP_K):
        pltpu.make_async_copy(hp_ref, _slab(xs_ref, 0, slab_rows, tm), sem).wait()


def _dispatch(hp, dest_flat, zstart, tail, p_rows, slab_rows):
    n = hp.shape[0] // slab_rows
    tm = min(DISPATCH_TM, n)
    assert n % tm == 0 and tm % DISPATCH_UNROLL == 0
    n_experts = zstart.shape[0]
    dest3 = dest_flat.reshape(n // tm, 1, tm * TOP_K)
    kernel = functools.partial(_dispatch_kernel, tm=tm, n_experts=n_experts, slab_rows=slab_rows)
    return pl.pallas_call(
        kernel,
        grid_spec=pltpu.PrefetchScalarGridSpec(
            num_scalar_prefetch=2,
            grid=(n // tm,),
            in_specs=[pl.BlockSpec((1, 1, tm * TOP_K), lambda i, *_: (i, 0, 0),
                                   memory_space=pltpu.SMEM),
                      pl.BlockSpec((tm * slab_rows, LANES), lambda i, *_: (i, 0))],
            out_specs=pl.BlockSpec(memory_space=pl.ANY),
            scratch_shapes=[pltpu.VMEM((SUB * slab_rows, LANES), jnp.uint32),
                            pltpu.SemaphoreType.DMA(()),
                            pltpu.SemaphoreType.DMA(())]),
        out_shape=jax.ShapeDtypeStruct((p_rows * slab_rows, LANES), jnp.uint32),
        compiler_params=_cparams(("arbitrary",)),
        name="dispatch",
    )(zstart, tail, dest3, hp)


def _expert_kernel(vexp_ref, vrow_ref, vnsub_ref, tail_ref,
                   xs_ref, wg_ref, wu_ref, wd_ref, bg_ref, bu_ref, bd_ref, ys_ref,
                   xb, acc, wgb, wub, wdb, xstage, ostage, xsem, osem, zsem, *, n_f):
    del vexp_ref
    v = pl.program_id(0)
    j = pl.program_id(1)
    d = xb.shape[1]
    slab_rows = d // (2 * LANES)
    stage_rows = SUB * slab_rows
    f32, bf16 = jnp.float32, jnp.bfloat16
    nsub = vnsub_ref[v]
    row0 = vrow_ref[v]

    def stage_slot(ref, slot):
        return ref.at[pl.ds(pl.multiple_of(slot * stage_rows, stage_rows), stage_rows)]

    def x_copy(first_row, s):
        slot = s % X_SLOTS
        return pltpu.make_async_copy(_slab(xs_ref, first_row + s * SUB, slab_rows, SUB),
                                     stage_slot(xstage, slot), xsem.at[slot])

    def x_prefetch(first_row, count):
        for s in range(X_AHEAD):
            @pl.when(s < count)
            def _():
                x_copy(first_row, s).start()

    def o_copy(s, slot):
        return pltpu.make_async_copy(stage_slot(ostage, slot),
                                     _slab(ys_ref, row0 + s * SUB, slab_rows, SUB), osem.at[slot])

    @pl.when((v == 0) & (j == 0))
    def _():
        ostage[0:stage_rows, :] = jnp.zeros((stage_rows, LANES), jnp.uint32)

        def z_copy(t):
            return pltpu.make_async_copy(stage_slot(ostage, 0),
                                         _slab(ys_ref, tail_ref[0] + t * SUB, slab_rows, SUB), zsem)

        def z_start(t, carry):
            z_copy(t).start()
            return carry

        def z_wait(t, carry):
            z_copy(t).wait()
            return carry
        lax.fori_loop(0, tail_ref[1], z_start, 0)
        lax.fori_loop(0, tail_ref[1], z_wait, 0)
        x_prefetch(row0, nsub)

    @pl.when(j == 0)
    def _():
        def body(s, carry):
            slot = s % X_SLOTS

            @pl.when(s + X_AHEAD < nsub)
            def _():
                x_copy(row0, s + X_AHEAD).start()
            x_copy(row0, s).wait()
            base = pl.multiple_of(s * SUB, SUB)
            for jj in range(slab_rows):
                lo, hi = _unpack_slab_words(
                    xstage[pl.ds(slot * stage_rows + jj, SUB, stride=slab_rows), :])
                xb[pl.ds(base, SUB), jj * LANES:(jj + 1) * LANES] = lo.astype(bf16)
                xb[pl.ds(base, SUB), d // 2 + jj * LANES:d // 2 + (jj + 1) * LANES] = hi.astype(bf16)
            acc[pl.ds(base, SUB), :] = jnp.zeros((SUB, d), f32)
            return carry
        lax.fori_loop(0, nsub, body, 0)

    @pl.when(nsub > 0)
    def _():
        wgb[...] = wg_ref[...].astype(bf16)
        wub[...] = wu_ref[...].astype(bf16)
        wdb[...] = wd_ref[...].astype(bf16)

        def chain(base, m):
            rows = pl.ds(base, m)
            xt = xb[rows, :]
            g = jnp.dot(xt, wgb[...], preferred_element_type=f32) + bg_ref[...]
            u = jnp.dot(xt, wub[...], preferred_element_type=f32) + bu_ref[...]
            g = jnp.minimum(g, SWIGLU_LIMIT)
            u = jnp.clip(u, -SWIGLU_LIMIT, SWIGLU_LIMIT)
            a = (u + 1.0) * (g * jax.nn.sigmoid(SWIGLU_ALPHA * g))
            acc[rows, :] += jnp.dot(a.astype(bf16), wdb[...], preferred_element_type=f32)

        def quad(q, carry):
            base = pl.multiple_of(q * (4 * SUB), 4 * SUB)
            chain(base, 2 * SUB)
            chain(pl.multiple_of(base + 2 * SUB, 2 * SUB), 2 * SUB)
            return carry
        lax.fori_loop(0, nsub // 4, quad, 0)

        @pl.when(nsub % 4 >= 2)
        def _():
            chain(pl.multiple_of((nsub // 4) * (4 * SUB), 2 * SUB), 2 * SUB)

        @pl.when(nsub % 2 == 1)
        def _():
            chain(pl.multiple_of((nsub - 1) * SUB, SUB), SUB)

    @pl.when((j == n_f - 1) & (v + 1 < pl.num_programs(0)))
    def _():
        nxt = jnp.minimum(v + 1, pl.num_programs(0) - 1)
        x_prefetch(vrow_ref[nxt], vnsub_ref[nxt])

    @pl.when((j == n_f - 1) & (nsub > 0))
    def _():
        def body(s, carry):
            slot = s % 2

            @pl.when(s >= 2)
            def _():
                o_copy(s - 2, slot).wait()
            words = _pack_slab_words(acc[pl.ds(pl.multiple_of(s * SUB, SUB), SUB), :] + bd_ref[...])
            for jj in range(slab_rows):
                ostage[pl.ds(slot * stage_rows + jj, SUB, stride=slab_rows), :] = (
                    words[:, jj * LANES:(jj + 1) * LANES])
            o_copy(s, slot).start()
            return carry
        lax.fori_loop(0, nsub, body, 0)

        @pl.when(nsub >= 2)
        def _():
            o_copy(nsub - 2, nsub % 2).wait()
        o_copy(nsub - 1, (nsub - 1) % 2).wait()


def _expert(xs, w_gate, b_gate, w_up, b_up, w_down, b_down, vexp, vrow, vnsub, tail, p_rows):
    n_experts, d, f = w_gate.shape
    tf = min(EXPERT_TF, f)
    assert f % tf == 0 and EXPERT_ROWS % (2 * SUB) == 0
    n_f = f // tf
    slab_rows = d // (2 * LANES)
    n_visits = vexp.shape[0]

    def w_idx(v, j, ve, vr, vn, tl):
        return jnp.where(vn[v] == 0, n_f - 1, j)

    kernel = functools.partial(_expert_kernel, n_f=n_f)
    return pl.pallas_call(
        kernel,
        grid_spec=pltpu.PrefetchScalarGridSpec(
            num_scalar_prefetch=4,
            grid=(n_visits, n_f),
            in_specs=[
                pl.BlockSpec(memory_space=pl.ANY),
                pl.BlockSpec((None, d, tf), lambda v, j, ve, *s: (ve[v], 0, w_idx(v, j, ve, *s))),
                pl.BlockSpec((None, d, tf), lambda v, j, ve, *s: (ve[v], 0, w_idx(v, j, ve, *s))),
                pl.BlockSpec((None, tf, d), lambda v, j, ve, *s: (ve[v], w_idx(v, j, ve, *s), 0)),
                pl.BlockSpec((None, 1, tf), lambda v, j, ve, *s: (ve[v], 0, w_idx(v, j, ve, *s))),
                pl.BlockSpec((None, 1, tf), lambda v, j, ve, *s: (ve[v], 0, w_idx(v, j, ve, *s))),
                pl.BlockSpec((None, 1, d), lambda v, j, ve, *s: (ve[v], 0, 0)),
            ],
            out_specs=pl.BlockSpec(memory_space=pl.ANY),
            scratch_shapes=[pltpu.VMEM((EXPERT_ROWS, d), jnp.bfloat16),
                            pltpu.VMEM((EXPERT_ROWS, d), jnp.float32),
                            pltpu.VMEM((d, tf), jnp.bfloat16),
                            pltpu.VMEM((d, tf), jnp.bfloat16),
                            pltpu.VMEM((tf, d), jnp.bfloat16),
                            pltpu.VMEM((X_SLOTS * SUB * slab_rows, LANES), jnp.uint32),
                            pltpu.VMEM((2 * SUB * slab_rows, LANES), jnp.uint32),
                            pltpu.SemaphoreType.DMA((X_SLOTS,)),
                            pltpu.SemaphoreType.DMA((2,)),
                            pltpu.SemaphoreType.DMA(())]),
        out_shape=jax.ShapeDtypeStruct((p_rows * slab_rows, LANES), jnp.uint32),
        compiler_params=_cparams(("arbitrary", "arbitrary")),
        name="expert",
    )(vexp, vrow, vnsub, tail, xs, w_gate, w_up, w_down,
      b_gate.reshape(n_experts, 1, f), b_up.reshape(n_experts, 1, f),
      b_down.reshape(n_experts, 1, d))


def _combine_kernel(dest_ref, dnext_ref, ys_ref, wgt_ref, x1_ref, mod_ref, g_ref, o_ref,
                    buf, ybuf, sem, *, tm):
    i = pl.program_id(0)
    d = x1_ref.shape[1]
    slab_rows = d // (2 * LANES)
    half = TOP_K * tm

    def gather(idx_ref, slot):
        def body(t, carry):
            for u in range(COMBINE_UNROLL):
                tok = t * COMBINE_UNROLL + u
                for k in range(TOP_K):
                    src = _slab(ys_ref, idx_ref[0, 0, tok * TOP_K + k], slab_rows)
                    dst = _slab(buf, slot * half + k * tm + tok, slab_rows)
                    pltpu.make_async_copy(src, dst, sem.at[slot]).start(priority=k % 2)
            return carry
        lax.fori_loop(0, tm // COMBINE_UNROLL, body, 0)

    slot = i % 2

    @pl.when(i == 0)
    def _():
        gather(dest_ref, 0)

    @pl.when(i + 1 < pl.num_programs(0))
    def _():
        gather(dnext_ref, 1 - slot)

    pltpu.make_async_copy(_slab(ys_ref, 0, slab_rows, half), _slab(buf, slot * half, slab_rows, half),
                          sem.at[slot]).wait()

    wgt = wgt_ref[...]
    base = slot * half * slab_rows
    for jj in range(slab_rows):
        ylo = jnp.zeros((tm, LANES), jnp.float32)
        yhi = jnp.zeros((tm, LANES), jnp.float32)
        for k in range(TOP_K):
            lo, hi = _unpack_slab_words(
                buf[pl.ds(base + k * tm * slab_rows + jj, tm, stride=slab_rows), :])
            ylo = ylo + wgt[:, k:k + 1] * lo
            yhi = yhi + wgt[:, k:k + 1] * hi
        ybuf[:, jj * LANES:(jj + 1) * LANES] = ylo
        ybuf[:, d // 2 + jj * LANES:d // 2 + (jj + 1) * LANES] = yhi
    mod = mod_ref[0]
    o_ref[...] = x1_ref[...] + mod[5:6] * _rms(ybuf[...], g_ref[...])


def _combine(ys, dest_flat, wgt, x1, mod3, seq, g_post):
    n, d = x1.shape
    slab_rows = d // (2 * LANES)
    tm = COMBINE_TM
    assert seq % tm == 0
    tiles_per_seq = seq // tm
    n_tiles = n // tm
    dest3 = dest_flat.reshape(n_tiles, 1, tm * TOP_K)
    kernel = functools.partial(_combine_kernel, tm=tm)
    return pl.pallas_call(
        kernel,
        grid=(n_tiles,),
        in_specs=[pl.BlockSpec((1, 1, tm * TOP_K), lambda i: (i, 0, 0), memory_space=pltpu.SMEM),
                  pl.BlockSpec((1, 1, tm * TOP_K), lambda i: (jnp.minimum(i + 1, n_tiles - 1), 0, 0),
                               memory_space=pltpu.SMEM),
                  pl.BlockSpec(memory_space=pl.ANY),
                  pl.BlockSpec((tm, LANES), lambda i: (i, 0)),
                  pl.BlockSpec((tm, d), lambda i: (i, 0)),
                  pl.BlockSpec((1, mod3.shape[1], d), lambda i: (i // tiles_per_seq, 0, 0)),
                  pl.BlockSpec((1, d), lambda i: (0, 0))],
        out_specs=pl.BlockSpec((tm, d), lambda i: (i, 0)),
        out_shape=jax.ShapeDtypeStruct((n, d), jnp.float32),
        scratch_shapes=[pltpu.VMEM((2 * TOP_K * tm * slab_rows, LANES), jnp.uint32),
                        pltpu.VMEM((tm, d), jnp.float32),
                        pltpu.SemaphoreType.DMA((2,))],
        compiler_params=_cparams(("arbitrary",)),
        name="combine",
    )(dest3, dest3, ys, wgt, x1, mod3, g_post)


def _routing_tables(counts, n_assign):
    n_experts = counts.shape[0]
    p_rows = n_assign + n_experts * SUB
    n_visits = n_experts + -(-p_rows // EXPERT_ROWS)

    padded = jnp.maximum((counts + SUB - 1) // SUB, 1) * SUB
    e_ids = jnp.arange(n_experts, dtype=jnp.int32)
    lower = e_ids[None, :] <= e_ids[:, None]
    pad_end = jnp.sum(jnp.where(lower, padded[None, :], 0), axis=1)
    pad_start = pad_end - padded
    total = pad_end[-1]
    zstart = (pad_start + counts).astype(jnp.int32)
    tail_xs = jnp.stack([total, (p_rows + SUB - total) // SUB]).astype(jnp.int32)
    tail_ys = jnp.stack([total, (p_rows - total) // SUB]).astype(jnp.int32)

    n_chunk = (padded + EXPERT_ROWS - 1) // EXPERT_ROWS
    chunk_end = jnp.sum(jnp.where(lower, n_chunk[None, :], 0), axis=1)
    visit = jnp.arange(n_visits, dtype=jnp.int32)
    used = visit < chunk_end[-1]
    vexp = jnp.minimum(jnp.sum((chunk_end[None, :] <= visit[:, None]).astype(jnp.int32), axis=1),
                       n_experts - 1)
    onehot = vexp[:, None] == e_ids[None, :]
    pick = lambda a: jnp.sum(jnp.where(onehot, a[None, :], 0), axis=1)
    chunk = visit - pick(chunk_end - n_chunk)
    vrow = jnp.where(used, pick(pad_start) + chunk * EXPERT_ROWS, 0)
    vnsub = jnp.where(used, jnp.minimum(pick(padded) - chunk * EXPERT_ROWS, EXPERT_ROWS) // SUB, 0)
    return (pad_start.astype(jnp.int32), zstart, tail_xs, tail_ys, vexp.astype(jnp.int32),
            vrow.astype(jnp.int32), vnsub.astype(jnp.int32), p_rows)


def kernel(x, c, w_ada, b_ada, g_pre_mix, g_post_mix, w_in, conv_a_w, conv_b_w, conv_b_b, ln_b_g, ln_b_b, w_out, g_pre_ffn, g_post_ffn, w_router, b_router, w_gate, b_gate, w_up, b_up, w_down, b_down):
    bsz, seq, d = x.shape
    n = bsz * seq
    depth = w_ada.shape[0]
    n_experts = w_router.shape[-1]
    assert n_experts <= LANES
    xf = x.reshape(n, d)
    for l in range(depth):
        mod = _ada(c, w_ada[l], b_ada[l])
        n_mod = mod.shape[1] // d
        mod3 = mod.reshape(bsz, n_mod, d)

        wr = jnp.zeros((d, LANES), jnp.float32).at[:, :n_experts].set(w_router[l])
        wr_hi = wr.astype(jnp.bfloat16)
        wr_lo = (wr - wr_hi.astype(jnp.float32)).astype(jnp.bfloat16)
        br = jnp.full((1, LANES), NEG_BIG, jnp.float32).at[0, :n_experts].set(b_router[l])

        x1, hp, idx, wgt, rank, cnt = _mix(
            xf, mod3, seq, g_pre_mix[l].reshape(1, d), g_post_mix[l].reshape(1, d),
            g_pre_ffn[l].reshape(1, d), w_in[l].astype(jnp.bfloat16), conv_a_w[l], conv_b_w[l],
            conv_b_b[l].reshape(1, -1), ln_b_g[l].reshape(1, -1), ln_b_b[l].reshape(1, -1),
            w_out[l].astype(jnp.bfloat16), jnp.concatenate([wr_hi, wr_lo], axis=1), wr_hi, br)

        (pad_start, zstart, tail_xs, tail_ys, vexp, vrow, vnsub,
         p_rows) = _routing_tables(cnt[0, :n_experts], n * TOP_K)
        start_row = jnp.zeros((1, LANES), jnp.float32).at[0, :n_experts].set(
            pad_start.astype(jnp.float32))
        dest = _plan(idx, rank, start_row)
        dest_flat = dest[:, :TOP_K].reshape(n * TOP_K)

        slab_rows = d // (2 * LANES)
        xs = _dispatch(hp, dest_flat, zstart, tail_xs, p_rows + SUB, slab_rows)
        ys = _expert(xs, w_gate[l], b_gate[l], w_up[l], b_up[l], w_down[l], b_down[l],
                     vexp, vrow, vnsub, tail_ys, p_rows)
        xf = _combine(ys, dest_flat, wgt, x1, mod3, seq, g_post_ffn[l].reshape(1, d))
    return xf.reshape(bsz, seq, d)
```

```python
import functools

import jax
import jax.numpy as jnp
from jax import lax
from jax.experimental import pallas as pl
from jax.experimental.pallas import tpu as pltpu

EPS = 1e-6
TOP_K = 4
CONV_A = 3
CONV_B = 31
SWIGLU_ALPHA = 1.702
SWIGLU_LIMIT = 7.0

LANES = 128
SUBLANES = 8
VMEM_LIMIT_BYTES = 56 * 1024 * 1024

ADA_TN = 1024
MIX_TM = 256
HALO_A = 8
HALO_B = 32
PLAN_TM = 2048
DISPATCH_TM = 1024
DISPATCH_UNROLL = 8
SUB = 128
EXPERT_ROWS = 2304
EXPERT_TF = 256
CHAIN_MAX = 16
X_AHEAD = 5
X_SLOTS = X_AHEAD + 1
O_SLOTS = 4
COMBINE_TM = 256
COMBINE_UNROLL = 8
NEG_BIG = -1e30


def _cparams(sem):
    return pltpu.CompilerParams(dimension_semantics=sem, vmem_limit_bytes=VMEM_LIMIT_BYTES)


def _rms(x, g):
    return x * lax.rsqrt(jnp.mean(x * x, axis=-1, keepdims=True) + EPS) * g


def _pack_slab_words(v):
    half = v.shape[1] // 2
    lo = pltpu.bitcast(v[:, :half].astype(jnp.bfloat16).astype(jnp.float32), jnp.uint32)
    hi = pltpu.bitcast(v[:, half:].astype(jnp.bfloat16).astype(jnp.float32), jnp.uint32)
    return (lo >> 16) | (hi & jnp.uint32(0xFFFF0000))


def _unpack_slab_words(w):
    lo = pltpu.bitcast(w << 16, jnp.float32)
    hi = pltpu.bitcast(w & jnp.uint32(0xFFFF0000), jnp.float32)
    return lo, hi


def _ada_kernel(c_ref, w_ref, b_ref, o_ref):
    c = c_ref[...]
    ca = (c * jax.nn.sigmoid(c)).astype(jnp.bfloat16)
    o_ref[...] = jnp.dot(ca, w_ref[...].astype(jnp.bfloat16),
                         preferred_element_type=jnp.float32) + b_ref[...]


def _ada(c, w_ada, b_ada):
    bsz, d = c.shape
    n_out = w_ada.shape[1]
    tn = min(ADA_TN, n_out)
    assert n_out % tn == 0 and bsz <= SUBLANES
    c8 = jnp.zeros((SUBLANES, d), jnp.float32).at[:bsz].set(c)
    out = pl.pallas_call(
        _ada_kernel,
        grid=(n_out // tn,),
        in_specs=[pl.BlockSpec((SUBLANES, d), lambda j: (0, 0)),
                  pl.BlockSpec((d, tn), lambda j: (0, j)),
                  pl.BlockSpec((1, tn), lambda j: (0, j))],
        out_specs=pl.BlockSpec((SUBLANES, tn), lambda j: (0, j)),
        out_shape=jax.ShapeDtypeStruct((SUBLANES, n_out), jnp.float32),
        compiler_params=_cparams(("arbitrary",)),
        name="ada",
    )(c8, w_ada, b_ada.reshape(1, n_out))
    return out[:bsz]


def _mix_kernel(x_ref, xp_ref, mod_ref, modp_ref, gpre_ref, gpost_ref, gffn_ref, win_ref,
                caw_ref, cbw_ref, cbb_ref, lng_ref, lnb_ref, wout_ref, wrc_ref, wrh_ref, br_ref,
                x1_ref, hp_ref, idx_ref, wgt_ref, rank_ref, cnt_ref,
                pa_buf, u_buf, v_buf, ycat, run_cnt, *, tiles_per_seq, wa, wb):
    i = pl.program_id(0)
    tm, d = x_ref.shape
    f32, bf16 = jnp.float32, jnp.bfloat16

    @pl.when(i == 0)
    def _():
        run_cnt[...] = jnp.zeros_like(run_cnt)
        ycat[...] = jnp.zeros_like(ycat)

    @pl.when(i % tiles_per_seq == 0)
    def _():
        pa_buf[:, 0:HALO_A, :] = jnp.zeros((wa // LANES, HALO_A, LANES), f32)
        u_buf[:, 0:HALO_B, :] = jnp.zeros((wb // LANES, HALO_B, LANES), f32)

    y_prev = jnp.dot(ycat[...], wout_ref[...], preferred_element_type=f32)

    x = x_ref[...]
    mod = mod_ref[0]
    h = (_rms(x, gpre_ref[...]) * (1.0 + mod[1:2]) + mod[0:1]).astype(bf16)
    proj_b = jnp.dot(h, win_ref[:, 3 * wa:3 * wa + 2 * wb], preferred_element_type=f32)
    proj_a = jnp.dot(h, win_ref[:, 0:3 * wa], preferred_element_type=f32)

    for cb in range(wa // LANES):
        sl = slice(cb * LANES, (cb + 1) * LANES)
        pa = proj_a[:, sl] * proj_a[:, 2 * wa + cb * LANES:2 * wa + (cb + 1) * LANES]
        pa_buf[cb, HALO_A:HALO_A + tm, :] = pa
        conv = caw_ref[CONV_A - 1:CONV_A, sl] * pa
        for k in range(CONV_A - 1):
            off = HALO_A - (CONV_A - 1) + k
            conv = conv + caw_ref[k:k + 1, sl] * pa_buf[cb, off:off + tm, :]
        ycat[:, sl] = (proj_a[:, wa + cb * LANES:wa + (cb + 1) * LANES] * conv).astype(bf16)
        pa_buf[cb, 0:HALO_A, :] = pa_buf[cb, tm:tm + HALO_A, :]

    for cb in range(wb // LANES):
        sl = slice(cb * LANES, (cb + 1) * LANES)
        u = proj_b[:, sl] * jax.nn.sigmoid(proj_b[:, wb + cb * LANES:wb + (cb + 1) * LANES])
        u_buf[cb, HALO_B:HALO_B + tm, :] = u
        acc = cbb_ref[:, sl] + cbw_ref[CONV_B - 1:CONV_B, sl] * u
        for k in range(CONV_B - 1):
            off = HALO_B - (CONV_B - 1) + k
            acc = acc + cbw_ref[k:k + 1, sl] * u_buf[cb, off:off + tm, :]
        v_buf[:, sl] = acc
        u_buf[cb, 0:HALO_B, :] = u_buf[cb, tm:tm + HALO_B, :]
    v = v_buf[...]
    mu = jnp.mean(v, axis=-1, keepdims=True)
    vc = v - mu
    var = jnp.mean(vc * vc, axis=-1, keepdims=True)
    yb = vc * lax.rsqrt(var + EPS) * lng_ref[...] + lnb_ref[...]
    ycat[:, wa:wa + wb] = (yb * jax.nn.sigmoid(yb)).astype(bf16)

    _mix_stage2(i >= 1, y_prev, xp_ref, modp_ref, gpost_ref, gffn_ref, wrc_ref, wrh_ref, br_ref,
                x1_ref, hp_ref, idx_ref, wgt_ref, rank_ref, cnt_ref, run_cnt)


def _mix_stage2(valid, y, x_ref, mod_ref, gpost_ref, gffn_ref, wrc_ref, wrh_ref, br_ref,
                x1_ref, hp_ref, idx_ref, wgt_ref, rank_ref, cnt_ref, run_cnt):
    tm, d = x_ref.shape
    f32, bf16 = jnp.float32, jnp.bfloat16
    x = x_ref[...]
    mod = mod_ref[0]
    x1 = x + mod[2:3] * _rms(y, gpost_ref[...])
    x1_ref[...] = x1

    h2 = _rms(x1, gffn_ref[...]) * (1.0 + mod[4:5]) + mod[3:4]
    words = _pack_slab_words(h2)
    slab_rows = d // (2 * LANES)
    for j in range(slab_rows):
        hp_ref[pl.ds(j, tm, stride=slab_rows), :] = words[:, j * LANES:(j + 1) * LANES]

    h2_hi = h2.astype(bf16)
    h2_lo = (h2 - h2_hi.astype(f32)).astype(bf16)
    both = jnp.dot(h2_hi, wrc_ref[...], preferred_element_type=f32)
    logits = (both[:, :LANES] + both[:, LANES:]
              + jnp.dot(h2_lo, wrh_ref[...], preferred_element_type=f32)
              + br_ref[...])

    lane = lax.broadcasted_iota(jnp.int32, (tm, LANES), 1)
    lane_f = lane.astype(f32)
    vals, idxs = [], []
    cur = logits
    for _ in range(TOP_K):
        m = jnp.max(cur, axis=-1, keepdims=True)
        ix = jnp.min(jnp.where(cur == m, lane_f, float(LANES)), axis=-1,
                     keepdims=True).astype(jnp.int32)
        vals.append(m)
        idxs.append(ix)
        cur = jnp.where(lane == ix, -jnp.inf, cur)
    exps = [jnp.exp(vk - vals[0]) for vk in vals]
    denom = exps[0]
    for ek in exps[1:]:
        denom = denom + ek
    inv = 1.0 / denom

    row = lax.broadcasted_iota(jnp.int32, (tm, tm), 0)
    col = lax.broadcasted_iota(jnp.int32, (tm, tm), 1)
    tri = jnp.where(col < row, 1.0, 0.0).astype(bf16)
    run = run_cnt[...]
    idx_out = jnp.zeros((tm, LANES), jnp.int32)
    wgt_out = jnp.zeros((tm, LANES), f32)
    rank_out = jnp.zeros((tm, LANES), f32)
    for k in range(TOP_K):
        oh = jnp.where(lane == idxs[k], 1.0, 0.0)
        before = jnp.dot(tri, oh.astype(bf16), preferred_element_type=f32)
        rank_k = jnp.sum(oh * (before + run), axis=-1, keepdims=True)
        run = run + jnp.sum(oh, axis=0, keepdims=True)
        idx_out = jnp.where(lane == k, idxs[k], idx_out)
        wgt_out = jnp.where(lane == k, exps[k] * inv, wgt_out)
        rank_out = jnp.where(lane == k, rank_k, rank_out)
    run = jnp.where(valid, run, run_cnt[...])
    run_cnt[...] = run
    idx_ref[...] = idx_out
    wgt_ref[...] = wgt_out
    rank_ref[...] = rank_out.astype(jnp.int32)
    cnt_ref[...] = jnp.broadcast_to(run, cnt_ref.shape).astype(jnp.int32)


def _mix(x2, mod3, seq, g_pre, g_post, g_ffn, w_in, conv_a_w, conv_b_w, conv_b_b, ln_g, ln_b,
         w_out, wr_cat, wr_hi, br):
    n, d = x2.shape
    wa = conv_a_w.shape[1]
    wb = conv_b_w.shape[1]
    tm = MIX_TM
    assert seq % tm == 0 and tm >= HALO_B and wa % LANES == 0 and wb % LANES == 0
    assert wa + wb == d and d % (2 * LANES) == 0
    tiles_per_seq = seq // tm
    n_tiles = n // tm
    slab_rows = d // (2 * LANES)
    cur = lambda i: jnp.minimum(i, n_tiles - 1)
    prev = lambda i: jnp.maximum(i - 1, 0)
    const = lambda shape: pl.BlockSpec(shape, lambda i: (0,) * len(shape))
    resident = lambda shape: pl.BlockSpec(shape, lambda i: (0,) * len(shape),
                                          pipeline_mode=pl.Buffered(1))
    out_block = lambda w: pl.BlockSpec((tm, w), lambda i: (prev(i), 0))
    mod_block = lambda tile: pl.BlockSpec((1, mod3.shape[1], d),
                                          lambda i: (tile(i) // tiles_per_seq, 0, 0))
    kernel = functools.partial(_mix_kernel, tiles_per_seq=tiles_per_seq, wa=wa, wb=wb)
    return pl.pallas_call(
        kernel,
        grid=(n_tiles + 1,),
        in_specs=[pl.BlockSpec((tm, d), lambda i: (cur(i), 0)),
                  pl.BlockSpec((tm, d), lambda i: (prev(i), 0)),
                  mod_block(cur), mod_block(prev),
                  const((1, d)), const((1, d)), const((1, d)),
                  resident(w_in.shape),
                  const(conv_a_w.shape), const(conv_b_w.shape),
                  const((1, wb)), const((1, wb)), const((1, wb)),
                  resident(w_out.shape),
                  const(wr_cat.shape), const(wr_hi.shape), const((1, LANES))],
        out_specs=[out_block(d),
                   pl.BlockSpec((tm * slab_rows, LANES), lambda i: (prev(i), 0)),
                   out_block(LANES), out_block(LANES), out_block(LANES),
                   const((SUBLANES, LANES))],
        out_shape=[jax.ShapeDtypeStruct((n, d), jnp.float32),
                   jax.ShapeDtypeStruct((n * slab_rows, LANES), jnp.uint32),
                   jax.ShapeDtypeStruct((n, LANES), jnp.int32),
                   jax.ShapeDtypeStruct((n, LANES), jnp.float32),
                   jax.ShapeDtypeStruct((n, LANES), jnp.int32),
                   jax.ShapeDtypeStruct((SUBLANES, LANES), jnp.int32)],
        scratch_shapes=[pltpu.VMEM((wa // LANES, tm + HALO_A, LANES), jnp.float32),
                        pltpu.VMEM((wb // LANES, tm + HALO_B, LANES), jnp.float32),
                        pltpu.VMEM((tm, wb), jnp.float32),
                        pltpu.VMEM((tm, d), jnp.bfloat16),
                        pltpu.VMEM((1, LANES), jnp.float32)],
        compiler_params=_cparams(("arbitrary",)),
        name="mix",
    )(x2, x2, mod3, mod3, g_pre, g_post, g_ffn, w_in, conv_a_w, conv_b_w, conv_b_b, ln_g, ln_b,
      w_out, wr_cat, wr_hi, br)


def _plan_kernel(idx_ref, rank_ref, start_ref, dest_ref):
    tm = idx_ref.shape[0]
    lane = lax.broadcasted_iota(jnp.int32, (tm, LANES), 1)
    idx = idx_ref[...]
    rank = rank_ref[...]
    start = start_ref[...]
    out = jnp.zeros((tm, LANES), jnp.int32)
    for k in range(TOP_K):
        base = jnp.sum(jnp.where(lane == idx[:, k:k + 1], start, 0.0), axis=-1, keepdims=True)
        out = jnp.where(lane == k, base.astype(jnp.int32) + rank[:, k:k + 1], out)
    dest_ref[...] = out


def _plan(idx, rank, start_row):
    n = idx.shape[0]
    tm = min(PLAN_TM, n)
    assert n % tm == 0
    blk = pl.BlockSpec((tm, LANES), lambda i: (i, 0))
    return pl.pallas_call(
        _plan_kernel,
        grid=(n // tm,),
        in_specs=[blk, blk, pl.BlockSpec((1, LANES), lambda i: (0, 0))],
        out_specs=blk,
        out_shape=jax.ShapeDtypeStruct((n, LANES), jnp.int32),
        compiler_params=_cparams(("arbitrary",)),
        name="plan",
    )(idx, rank, start_row)


def _slab(ref, row, slab_rows, count=1):
    return ref.at[pl.ds(pl.multiple_of(row * slab_rows, slab_rows), count * slab_rows)]


def _dispatch_kernel(zstart_ref, tail_ref, dest_ref, hp_ref, xs_ref, zbuf, sem, zsem,
                     *, tm, n_experts, slab_rows):
    i = pl.program_id(0)

    def zero_copy(start):
        return pltpu.make_async_copy(zbuf, _slab(xs_ref, start, slab_rows, SUB), zsem)

    @pl.when(i == 0)
    def _():
        zbuf[...] = jnp.zeros_like(zbuf)
        for parity in range(2):
            for e in range(parity, n_experts, 2):
                zero_copy(zstart_ref[e]).start()
            for e in range(parity, n_experts, 2):
                zero_copy(zstart_ref[e]).wait()

        def tail_start(t, carry):
            zero_copy(tail_ref[0] + t * SUB).start()
            return carry

        def tail_wait(t, carry):
            zero_copy(tail_ref[0] + t * SUB).wait()
            return carry
        lax.fori_loop(0, tail_ref[1], tail_start, 0)
        lax.fori_loop(0, tail_ref[1], tail_wait, 0)

    def body(t, carry):
        for u in range(DISPATCH_UNROLL):
            tok = t * DISPATCH_UNROLL + u
            src = _slab(hp_ref, tok, slab_rows)
            for k in range(TOP_K):
                dst = _slab(xs_ref, dest_ref[0, 0, tok * TOP_K + k], slab_rows)
                pltpu.make_async_copy(src, dst, sem).start(priority=k % 2)
        return carry
    lax.fori_loop(0, tm // DISPATCH_UNROLL, body, 0)
    for k in range(TOP_K):
        pltpu.make_async_copy(hp_ref, _slab(xs_ref, 0, slab_rows, tm), sem).wait()


def _dispatch(hp, dest_flat, zstart, tail, p_rows, slab_rows):
    n = hp.shape[0] // slab_rows
    tm = min(DISPATCH_TM, n)
    assert n % tm == 0 and tm % DISPATCH_UNROLL == 0
    n_experts = zstart.shape[0]
    dest3 = dest_flat.reshape(n // tm, 1, tm * TOP_K)
    kernel = functools.partial(_dispatch_kernel, tm=tm, n_experts=n_experts, slab_rows=slab_rows)
    return pl.pallas_call(
        kernel,
        grid_spec=pltpu.PrefetchScalarGridSpec(
            num_scalar_prefetch=2,
            grid=(n // tm,),
            in_specs=[pl.BlockSpec((1, 1, tm * TOP_K), lambda i, *_: (i, 0, 0),
                                   memory_space=pltpu.SMEM),
                      pl.BlockSpec((tm * slab_rows, LANES), lambda i, *_: (i, 0))],
            out_specs=pl.BlockSpec(memory_space=pl.ANY),
            scratch_shapes=[pltpu.VMEM((SUB * slab_rows, LANES), jnp.uint32),
                            pltpu.SemaphoreType.DMA(()),
                            pltpu.SemaphoreType.DMA(())]),
        out_shape=jax.ShapeDtypeStruct((p_rows * slab_rows, LANES), jnp.uint32),
        compiler_params=_cparams(("arbitrary",)),
        name="dispatch",
    )(zstart, tail, dest3, hp)


def _expert_kernel(vexp_ref, vrow_ref, vnsub_ref, tail_ref,
                   xs_ref, wg_ref, wu_ref, wd_ref, bg_ref, bu_ref, bd_ref, ys_ref,
                   xb, acc, xstage, ostage, xsem, osem, zsem, *, n_f):
    del vexp_ref
    v = pl.program_id(0)
    j = pl.program_id(1)
    d = xb.shape[1]
    slab_rows = d // (2 * LANES)
    stage_rows = SUB * slab_rows
    f32, bf16 = jnp.float32, jnp.bfloat16
    nsub = vnsub_ref[v]
    row0 = vrow_ref[v]

    def stage_slot(ref, slot):
        return ref.at[pl.ds(pl.multiple_of(slot * stage_rows, stage_rows), stage_rows)]

    def x_copy(first_row, s):
        slot = s % X_SLOTS
        return pltpu.make_async_copy(_slab(xs_ref, first_row + s * SUB, slab_rows, SUB),
                                     stage_slot(xstage, slot), xsem.at[slot])

    def x_prefetch(first_row, count):
        for s in range(X_AHEAD):
            @pl.when(s < count)
            def _():
                x_copy(first_row, s).start()

    def o_copy(s, slot):
        return pltpu.make_async_copy(stage_slot(ostage, slot),
                                     _slab(ys_ref, row0 + s * SUB, slab_rows, SUB), osem.at[slot])

    @pl.when((v == 0) & (j == 0))
    def _():
        ostage[0:stage_rows, :] = jnp.zeros((stage_rows, LANES), jnp.uint32)

        def z_copy(t):
            return pltpu.make_async_copy(stage_slot(ostage, 0),
                                         _slab(ys_ref, tail_ref[0] + t * SUB, slab_rows, SUB), zsem)

        def z_start(t, carry):
            z_copy(t).start()
            return carry

        def z_wait(t, carry):
            z_copy(t).wait()
            return carry
        lax.fori_loop(0, tail_ref[1], z_start, 0)
        lax.fori_loop(0, tail_ref[1], z_wait, 0)
        x_prefetch(row0, nsub)

    @pl.when(j == 0)
    def _():
        def body(s, carry):
            slot = s % X_SLOTS

            @pl.when(s + X_AHEAD < nsub)
            def _():
                x_copy(row0, s + X_AHEAD).start()
            x_copy(row0, s).wait()
            base = pl.multiple_of(s * SUB, SUB)
            for jj in range(slab_rows):
                lo, hi = _unpack_slab_words(
                    xstage[pl.ds(slot * stage_rows + jj, SUB, stride=slab_rows), :])
                xb[pl.ds(base, SUB), jj * LANES:(jj + 1) * LANES] = lo.astype(bf16)
                xb[pl.ds(base, SUB), d // 2 + jj * LANES:d // 2 + (jj + 1) * LANES] = hi.astype(bf16)
            acc[pl.ds(base, SUB), :] = jnp.zeros((SUB, d), f32)
            return carry
        lax.fori_loop(0, nsub, body, 0)

    def chain(base, m):
        rows = pl.ds(base, m)
        xt = xb[rows, :]
        g = jnp.dot(xt, wg_ref[...].astype(bf16), preferred_element_type=f32) + bg_ref[...]
        u = jnp.dot(xt, wu_ref[...].astype(bf16), preferred_element_type=f32) + bu_ref[...]
        g = jnp.minimum(g, SWIGLU_LIMIT)
        u = jnp.clip(u, -SWIGLU_LIMIT, SWIGLU_LIMIT)
        a = (u + 1.0) * (g * jax.nn.sigmoid(SWIGLU_ALPHA * g))
        acc[rows, :] += jnp.dot(a.astype(bf16), wd_ref[...].astype(bf16),
                                preferred_element_type=f32)

    def big_body(q, carry):
        chain(pl.multiple_of(q * (CHAIN_MAX * SUB), CHAIN_MAX * SUB), CHAIN_MAX * SUB)
        return carry
    lax.fori_loop(0, nsub // CHAIN_MAX, big_body, 0)
    units = CHAIN_MAX // 2
    while units >= 1:
        @pl.when((nsub & units) != 0)
        def _(units=units):
            start = (nsub // (2 * units)) * (2 * units)
            chain(pl.multiple_of(start * SUB, units * SUB), units * SUB)
        units //= 2

    @pl.when((j == n_f - 1) & (v + 1 < pl.num_programs(0)))
    def _():
        nxt = jnp.minimum(v + 1, pl.num_programs(0) - 1)
        x_prefetch(vrow_ref[nxt], vnsub_ref[nxt])

    @pl.when((j == n_f - 1) & (nsub > 0))
    def _():
        def body(s, carry):
            slot = s % O_SLOTS

            @pl.when(s >= O_SLOTS)
            def _():
                o_copy(s - O_SLOTS, slot).wait()
            words = _pack_slab_words(acc[pl.ds(pl.multiple_of(s * SUB, SUB), SUB), :] + bd_ref[...])
            for jj in range(slab_rows):
                ostage[pl.ds(slot * stage_rows + jj, SUB, stride=slab_rows), :] = (
                    words[:, jj * LANES:(jj + 1) * LANES])
            o_copy(s, slot).start()
            return carry
        lax.fori_loop(0, nsub, body, 0)

        def drain(s, carry):
            o_copy(s, s % O_SLOTS).wait()
            return carry
        lax.fori_loop(jnp.maximum(nsub - O_SLOTS, 0), nsub, drain, 0)


def _expert(xs, w_gate, b_gate, w_up, b_up, w_down, b_down, vexp, vrow, vnsub, tail, p_rows):
    n_experts, d, f = w_gate.shape
    tf = min(EXPERT_TF, f)
    assert f % tf == 0 and EXPERT_ROWS % SUB == 0 and EXPERT_ROWS // SUB < 2 * CHAIN_MAX
    n_f = f // tf
    slab_rows = d // (2 * LANES)
    n_visits = vexp.shape[0]

    def w_idx(v, j, ve, vr, vn, tl):
        return jnp.where(vn[v] == 0, n_f - 1, j)

    kernel = functools.partial(_expert_kernel, n_f=n_f)
    return pl.pallas_call(
        kernel,
        grid_spec=pltpu.PrefetchScalarGridSpec(
            num_scalar_prefetch=4,
            grid=(n_visits, n_f),
            in_specs=[
                pl.BlockSpec(memory_space=pl.ANY),
                pl.BlockSpec((None, d, tf), lambda v, j, ve, *s: (ve[v], 0, w_idx(v, j, ve, *s))),
                pl.BlockSpec((None, d, tf), lambda v, j, ve, *s: (ve[v], 0, w_idx(v, j, ve, *s))),
                pl.BlockSpec((None, tf, d), lambda v, j, ve, *s: (ve[v], w_idx(v, j, ve, *s), 0)),
                pl.BlockSpec((None, 1, tf), lambda v, j, ve, *s: (ve[v], 0, w_idx(v, j, ve, *s))),
                pl.BlockSpec((None, 1, tf), lambda v, j, ve, *s: (ve[v], 0, w_idx(v, j, ve, *s))),
                pl.BlockSpec((None, 1, d), lambda v, j, ve, *s: (ve[v], 0, 0)),
            ],
            out_specs=pl.BlockSpec(memory_space=pl.ANY),
            scratch_shapes=[pltpu.VMEM((EXPERT_ROWS, d), jnp.bfloat16),
                            pltpu.VMEM((EXPERT_ROWS, d), jnp.float32),
                            pltpu.VMEM((X_SLOTS * SUB * slab_rows, LANES), jnp.uint32),
                            pltpu.VMEM((O_SLOTS * SUB * slab_rows, LANES), jnp.uint32),
                            pltpu.SemaphoreType.DMA((X_SLOTS,)),
                            pltpu.SemaphoreType.DMA((O_SLOTS,)),
                            pltpu.SemaphoreType.DMA(())]),
        out_shape=jax.ShapeDtypeStruct((p_rows * slab_rows, LANES), jnp.uint32),
        compiler_params=_cparams(("arbitrary", "arbitrary")),
        name="expert",
    )(vexp, vrow, vnsub, tail, xs, w_gate, w_up, w_down,
      b_gate.reshape(n_experts, 1, f), b_up.reshape(n_experts, 1, f),
      b_down.reshape(n_experts, 1, d))


def _combine_kernel(dest_ref, dnext_ref, ys_ref, wgt_ref, x1_ref, mod_ref, g_ref, o_ref,
                    buf, ybuf, sem, *, tm):
    i = pl.program_id(0)
    d = x1_ref.shape[1]
    slab_rows = d // (2 * LANES)
    half = TOP_K * tm

    def gather(idx_ref, slot):
        def body(t, carry):
            for u in range(COMBINE_UNROLL):
                tok = t * COMBINE_UNROLL + u
                for k in range(TOP_K):
                    src = _slab(ys_ref, idx_ref[0, 0, tok * TOP_K + k], slab_rows)
                    dst = _slab(buf, slot * half + k * tm + tok, slab_rows)
                    pltpu.make_async_copy(src, dst, sem.at[slot]).start(priority=k % 2)
            return carry
        lax.fori_loop(0, tm // COMBINE_UNROLL, body, 0)

    slot = i % 2

    @pl.when(i == 0)
    def _():
        gather(dest_ref, 0)

    @pl.when(i + 1 < pl.num_programs(0))
    def _():
        gather(dnext_ref, 1 - slot)

    pltpu.make_async_copy(_slab(ys_ref, 0, slab_rows, half), _slab(buf, slot * half, slab_rows, half),
                          sem.at[slot]).wait()

    wgt = wgt_ref[...]
    base = slot * half * slab_rows
    for jj in range(slab_rows):
        ylo = jnp.zeros((tm, LANES), jnp.float32)
        yhi = jnp.zeros((tm, LANES), jnp.float32)
        for k in range(TOP_K):
            lo, hi = _unpack_slab_words(
                buf[pl.ds(base + k * tm * slab_rows + jj, tm, stride=slab_rows), :])
            ylo = ylo + wgt[:, k:k + 1] * lo
            yhi = yhi + wgt[:, k:k + 1] * hi
        ybuf[:, jj * LANES:(jj + 1) * LANES] = ylo
        ybuf[:, d // 2 + jj * LANES:d // 2 + (jj + 1) * LANES] = yhi
    mod = mod_ref[0]
    o_ref[...] = x1_ref[...] + mod[5:6] * _rms(ybuf[...], g_ref[...])


def _combine(ys, dest_flat, wgt, x1, mod3, seq, g_post):
    n, d = x1.shape
    slab_rows = d // (2 * LANES)
    tm = COMBINE_TM
    assert seq % tm == 0
    tiles_per_seq = seq // tm
    n_tiles = n // tm
    dest3 = dest_flat.reshape(n_tiles, 1, tm * TOP_K)
    kernel = functools.partial(_combine_kernel, tm=tm)
    return pl.pallas_call(
        kernel,
        grid=(n_tiles,),
        in_specs=[pl.BlockSpec((1, 1, tm * TOP_K), lambda i: (i, 0, 0), memory_space=pltpu.SMEM),
                  pl.BlockSpec((1, 1, tm * TOP_K), lambda i: (jnp.minimum(i + 1, n_tiles - 1), 0, 0),
                               memory_space=pltpu.SMEM),
                  pl.BlockSpec(memory_space=pl.ANY),
                  pl.BlockSpec((tm, LANES), lambda i: (i, 0)),
                  pl.BlockSpec((tm, d), lambda i: (i, 0)),
                  pl.BlockSpec((1, mod3.shape[1], d), lambda i: (i // tiles_per_seq, 0, 0)),
                  pl.BlockSpec((1, d), lambda i: (0, 0))],
        out_specs=pl.BlockSpec((tm, d), lambda i: (i, 0)),
        out_shape=jax.ShapeDtypeStruct((n, d), jnp.float32),
        scratch_shapes=[pltpu.VMEM((2 * TOP_K * tm * slab_rows, LANES), jnp.uint32),
                        pltpu.VMEM((tm, d), jnp.float32),
                        pltpu.SemaphoreType.DMA((2,))],
        compiler_params=_cparams(("arbitrary",)),
        name="combine",
    )(dest3, dest3, ys, wgt, x1, mod3, g_post)


def _routing_tables(counts, n_assign):
    n_experts = counts.shape[0]
    p_rows = n_assign + n_experts * SUB
    n_visits = n_experts + -(-p_rows // EXPERT_ROWS)

    padded = jnp.maximum((counts + SUB - 1) // SUB, 1) * SUB
    e_ids = jnp.arange(n_experts, dtype=jnp.int32)
    lower = e_ids[None, :] <= e_ids[:, None]
    pad_end = jnp.sum(jnp.where(lower, padded[None, :], 0), axis=1)
    pad_start = pad_end - padded
    total = pad_end[-1]
    zstart = (pad_start + counts).astype(jnp.int32)
    tail_xs = jnp.stack([total, (p_rows + SUB - total) // SUB]).astype(jnp.int32)
    tail_ys = jnp.stack([total, (p_rows - total) // SUB]).astype(jnp.int32)

    n_chunk = (padded + EXPERT_ROWS - 1) // EXPERT_ROWS
    chunk_end = jnp.sum(jnp.where(lower, n_chunk[None, :], 0), axis=1)
    visit = jnp.arange(n_visits, dtype=jnp.int32)
    used = visit < chunk_end[-1]
    vexp = jnp.minimum(jnp.sum((chunk_end[None, :] <= visit[:, None]).astype(jnp.int32), axis=1),
                       n_experts - 1)
    onehot = vexp[:, None] == e_ids[None, :]
    pick = lambda a: jnp.sum(jnp.where(onehot, a[None, :], 0), axis=1)
    chunk = visit - pick(chunk_end - n_chunk)
    vrow = jnp.where(used, pick(pad_start) + chunk * EXPERT_ROWS, 0)
    vnsub = jnp.where(used, jnp.minimum(pick(padded) - chunk * EXPERT_ROWS, EXPERT_ROWS) // SUB, 0)
    return (pad_start.astype(jnp.int32), zstart, tail_xs, tail_ys, vexp.astype(jnp.int32),
            vrow.astype(jnp.int32), vnsub.astype(jnp.int32), p_rows)


def kernel(x, c, w_ada, b_ada, g_pre_mix, g_post_mix, w_in, conv_a_w, conv_b_w, conv_b_b, ln_b_g, ln_b_b, w_out, g_pre_ffn, g_post_ffn, w_router, b_router, w_gate, b_gate, w_up, b_up, w_down, b_down):
    bsz, seq, d = x.shape
    n = bsz * seq
    depth = w_ada.shape[0]
    n_experts = w_router.shape[-1]
    assert n_experts <= LANES
    xf = x.reshape(n, d)
    for l in range(depth):
        mod = _ada(c, w_ada[l], b_ada[l])
        n_mod = mod.shape[1] // d
        mod3 = mod.reshape(bsz, n_mod, d)

        wr = jnp.zeros((d, LANES), jnp.float32).at[:, :n_experts].set(w_router[l])
        wr_hi = wr.astype(jnp.bfloat16)
        wr_lo = (wr - wr_hi.astype(jnp.float32)).astype(jnp.bfloat16)
        br = jnp.full((1, LANES), NEG_BIG, jnp.float32).at[0, :n_experts].set(b_router[l])

        x1, hp, idx, wgt, rank, cnt = _mix(
            xf, mod3, seq, g_pre_mix[l].reshape(1, d), g_post_mix[l].reshape(1, d),
            g_pre_ffn[l].reshape(1, d), w_in[l].astype(jnp.bfloat16), conv_a_w[l], conv_b_w[l],
            conv_b_b[l].reshape(1, -1), ln_b_g[l].reshape(1, -1), ln_b_b[l].reshape(1, -1),
            w_out[l].astype(jnp.bfloat16), jnp.concatenate([wr_hi, wr_lo], axis=1), wr_hi, br)

        (pad_start, zstart, tail_xs, tail_ys, vexp, vrow, vnsub,
         p_rows) = _routing_tables(cnt[0, :n_experts], n * TOP_K)
        start_row = jnp.zeros((1, LANES), jnp.float32).at[0, :n_experts].set(
            pad_start.astype(jnp.float32))
        dest = _plan(idx, rank, start_row)
        dest_flat = dest[:, :TOP_K].reshape(n * TOP_K)

        slab_rows = d // (2 * LANES)
        xs = _dispatch(hp, dest_flat, zstart, tail_xs, p_rows + SUB, slab_rows)
        ys = _expert(xs, w_gate[l], b_gate[l], w_up[l], b_up[l], w_down[l], b_down[l],
                     vexp, vrow, vnsub, tail_ys, p_rows)
        xf = _combine(ys, dest_flat, wgt, x1, mod3, seq, g_post_ffn[l].reshape(1, d))
    return xf.reshape(bsz, seq, d)
```

```python
import functools

import jax
import jax.numpy as jnp
from jax import lax
from jax.experimental import pallas as pl
from jax.experimental.pallas import tpu as pltpu

EPS = 1e-6
TOP_K = 4
CONV_A = 3
CONV_B = 31
SWIGLU_ALPHA = 1.702
SWIGLU_LIMIT = 7.0

LANES = 128
SUBLANES = 8
VMEM_LIMIT_BYTES = 56 * 1024 * 1024

ADA_TN = 1024
MIX_TM = 256
HALO_A = 8
HALO_B = 32
PLAN_TM = 2048
DISPATCH_TM = 1024
DISPATCH_UNROLL = 8
SUB = 128
EXPERT_ROWS = 2304
EXPERT_TF = 256
CHAIN_MAX = 16
CHAIN_PIECE = 4
X_AHEAD = 5
X_SLOTS = X_AHEAD + 1
O_SLOTS = 4
COMBINE_TM = 256
COMBINE_UNROLL = 8
NEG_BIG = -1e30


def _cparams(sem):
    return pltpu.CompilerParams(dimension_semantics=sem, vmem_limit_bytes=VMEM_LIMIT_BYTES)


def _rms(x, g):
    return x * lax.rsqrt(jnp.mean(x * x, axis=-1, keepdims=True) + EPS) * g


def _pack_slab_words(v):
    half = v.shape[1] // 2
    lo = pltpu.bitcast(v[:, :half].astype(jnp.bfloat16).astype(jnp.float32), jnp.uint32)
    hi = pltpu.bitcast(v[:, half:].astype(jnp.bfloat16).astype(jnp.float32), jnp.uint32)
    return (lo >> 16) | (hi & jnp.uint32(0xFFFF0000))


def _unpack_slab_words(w):
    lo = pltpu.bitcast(w << 16, jnp.float32)
    hi = pltpu.bitcast(w & jnp.uint32(0xFFFF0000), jnp.float32)
    return lo, hi


def _ada_kernel(c_ref, w_ref, b_ref, o_ref):
    c = c_ref[...]
    ca = (c * jax.nn.sigmoid(c)).astype(jnp.bfloat16)
    o_ref[...] = jnp.dot(ca, w_ref[...].astype(jnp.bfloat16),
                         preferred_element_type=jnp.float32) + b_ref[...]


def _ada(c, w_ada, b_ada):
    bsz, d = c.shape
    n_out = w_ada.shape[1]
    tn = min(ADA_TN, n_out)
    assert n_out % tn == 0 and bsz <= SUBLANES
    c8 = jnp.zeros((SUBLANES, d), jnp.float32).at[:bsz].set(c)
    out = pl.pallas_call(
        _ada_kernel,
        grid=(n_out // tn,),
        in_specs=[pl.BlockSpec((SUBLANES, d), lambda j: (0, 0)),
                  pl.BlockSpec((d, tn), lambda j: (0, j)),
                  pl.BlockSpec((1, tn), lambda j: (0, j))],
        out_specs=pl.BlockSpec((SUBLANES, tn), lambda j: (0, j)),
        out_shape=jax.ShapeDtypeStruct((SUBLANES, n_out), jnp.float32),
        compiler_params=_cparams(("arbitrary",)),
        name="ada",
    )(c8, w_ada, b_ada.reshape(1, n_out))
    return out[:bsz]


def _mix_kernel(x_ref, xp_ref, mod_ref, modp_ref, gpre_ref, gpost_ref, gffn_ref, win_ref,
                caw_ref, cbw_ref, cbb_ref, lng_ref, lnb_ref, wout_ref, wrc_ref, wrh_ref, br_ref,
                x1_ref, hp_ref, idx_ref, wgt_ref, rank_ref, cnt_ref,
                pa_buf, u_buf, v_buf, ycat, run_cnt, *, tiles_per_seq, wa, wb):
    i = pl.program_id(0)
    tm, d = x_ref.shape
    f32, bf16 = jnp.float32, jnp.bfloat16

    @pl.when(i == 0)
    def _():
        run_cnt[...] = jnp.zeros_like(run_cnt)
        ycat[...] = jnp.zeros_like(ycat)

    @pl.when(i % tiles_per_seq == 0)
    def _():
        pa_buf[:, 0:HALO_A, :] = jnp.zeros((wa // LANES, HALO_A, LANES), f32)
        u_buf[:, 0:HALO_B, :] = jnp.zeros((wb // LANES, HALO_B, LANES), f32)

    y_prev = jnp.dot(ycat[...], wout_ref[...], preferred_element_type=f32)

    x = x_ref[...]
    mod = mod_ref[0]
    h = (_rms(x, gpre_ref[...]) * (1.0 + mod[1:2]) + mod[0:1]).astype(bf16)
    proj_b = jnp.dot(h, win_ref[:, 3 * wa:3 * wa + 2 * wb], preferred_element_type=f32)
    proj_a = jnp.dot(h, win_ref[:, 0:3 * wa], preferred_element_type=f32)

    for cb in range(wa // LANES):
        sl = slice(cb * LANES, (cb + 1) * LANES)
        pa = proj_a[:, sl] * proj_a[:, 2 * wa + cb * LANES:2 * wa + (cb + 1) * LANES]
        pa_buf[cb, HALO_A:HALO_A + tm, :] = pa
        conv = caw_ref[CONV_A - 1:CONV_A, sl] * pa
        for k in range(CONV_A - 1):
            off = HALO_A - (CONV_A - 1) + k
            conv = conv + caw_ref[k:k + 1, sl] * pa_buf[cb, off:off + tm, :]
        ycat[:, sl] = (proj_a[:, wa + cb * LANES:wa + (cb + 1) * LANES] * conv).astype(bf16)
        pa_buf[cb, 0:HALO_A, :] = pa_buf[cb, tm:tm + HALO_A, :]

    for cb in range(wb // LANES):
        sl = slice(cb * LANES, (cb + 1) * LANES)
        u = proj_b[:, sl] * jax.nn.sigmoid(proj_b[:, wb + cb * LANES:wb + (cb + 1) * LANES])
        u_buf[cb, HALO_B:HALO_B + tm, :] = u
        acc = cbb_ref[:, sl] + cbw_ref[CONV_B - 1:CONV_B, sl] * u
        for k in range(CONV_B - 1):
            off = HALO_B - (CONV_B - 1) + k
            acc = acc + cbw_ref[k:k + 1, sl] * u_buf[cb, off:off + tm, :]
        v_buf[:, sl] = acc
        u_buf[cb, 0:HALO_B, :] = u_buf[cb, tm:tm + HALO_B, :]
    v = v_buf[...]
    mu = jnp.mean(v, axis=-1, keepdims=True)
    vc = v - mu
    var = jnp.mean(vc * vc, axis=-1, keepdims=True)
    yb = vc * lax.rsqrt(var + EPS) * lng_ref[...] + lnb_ref[...]
    ycat[:, wa:wa + wb] = (yb * jax.nn.sigmoid(yb)).astype(bf16)

    _mix_stage2(i >= 1, y_prev, xp_ref, modp_ref, gpost_ref, gffn_ref, wrc_ref, wrh_ref, br_ref,
                x1_ref, hp_ref, idx_ref, wgt_ref, rank_ref, cnt_ref, run_cnt)


def _mix_stage2(valid, y, x_ref, mod_ref, gpost_ref, gffn_ref, wrc_ref, wrh_ref, br_ref,
                x1_ref, hp_ref, idx_ref, wgt_ref, rank_ref, cnt_ref, run_cnt):
    tm, d = x_ref.shape
    f32, bf16 = jnp.float32, jnp.bfloat16
    x = x_ref[...]
    mod = mod_ref[0]
    x1 = x + mod[2:3] * _rms(y, gpost_ref[...])
    x1_ref[...] = x1

    h2 = _rms(x1, gffn_ref[...]) * (1.0 + mod[4:5]) + mod[3:4]
    words = _pack_slab_words(h2)
    slab_rows = d // (2 * LANES)
    for j in range(slab_rows):
        hp_ref[pl.ds(j, tm, stride=slab_rows), :] = words[:, j * LANES:(j + 1) * LANES]

    h2_hi = h2.astype(bf16)
    h2_lo = (h2 - h2_hi.astype(f32)).astype(bf16)
    both = jnp.dot(h2_hi, wrc_ref[...], preferred_element_type=f32)
    logits = (both[:, :LANES] + both[:, LANES:]
              + jnp.dot(h2_lo, wrh_ref[...], preferred_element_type=f32)
              + br_ref[...])

    lane = lax.broadcasted_iota(jnp.int32, (tm, LANES), 1)
    lane_f = lane.astype(f32)
    vals, idxs = [], []
    cur = logits
    for _ in range(TOP_K):
        m = jnp.max(cur, axis=-1, keepdims=True)
        ix = jnp.min(jnp.where(cur == m, lane_f, float(LANES)), axis=-1,
                     keepdims=True).astype(jnp.int32)
        vals.append(m)
        idxs.append(ix)
        cur = jnp.where(lane == ix, -jnp.inf, cur)
    exps = [jnp.exp(vk - vals[0]) for vk in vals]
    denom = exps[0]
    for ek in exps[1:]:
        denom = denom + ek
    inv = 1.0 / denom

    row = lax.broadcasted_iota(jnp.int32, (tm, tm), 0)
    col = lax.broadcasted_iota(jnp.int32, (tm, tm), 1)
    tri = jnp.where(col < row, 1.0, 0.0).astype(bf16)
    run = run_cnt[...]
    idx_out = jnp.zeros((tm, LANES), jnp.int32)
    wgt_out = jnp.zeros((tm, LANES), f32)
    rank_out = jnp.zeros((tm, LANES), f32)
    onehots = [jnp.where(lane == idxs[k], 1.0, 0.0) for k in range(TOP_K)]
    before_all = jnp.dot(tri, jnp.concatenate(onehots, axis=1).astype(bf16),
                         preferred_element_type=f32)
    for k in range(TOP_K):
        oh = onehots[k]
        before = before_all[:, k * LANES:(k + 1) * LANES]
        rank_k = jnp.sum(oh * (before + run), axis=-1, keepdims=True)
        run = run + jnp.sum(oh, axis=0, keepdims=True)
        idx_out = jnp.where(lane == k, idxs[k], idx_out)
        wgt_out = jnp.where(lane == k, exps[k] * inv, wgt_out)
        rank_out = jnp.where(lane == k, rank_k, rank_out)
    run = jnp.where(valid, run, run_cnt[...])
    run_cnt[...] = run
    idx_ref[...] = idx_out
    wgt_ref[...] = wgt_out
    rank_ref[...] = rank_out.astype(jnp.int32)
    cnt_ref[...] = jnp.broadcast_to(run, cnt_ref.shape).astype(jnp.int32)


def _mix(x2, mod3, seq, g_pre, g_post, g_ffn, w_in, conv_a_w, conv_b_w, conv_b_b, ln_g, ln_b,
         w_out, wr_cat, wr_hi, br):
    n, d = x2.shape
    wa = conv_a_w.shape[1]
    wb = conv_b_w.shape[1]
    tm = MIX_TM
    assert seq % tm == 0 and tm >= HALO_B and wa % LANES == 0 and wb % LANES == 0
    assert wa + wb == d and d % (2 * LANES) == 0
    tiles_per_seq = seq // tm
    n_tiles = n // tm
    slab_rows = d // (2 * LANES)
    cur = lambda i: jnp.minimum(i, n_tiles - 1)
    prev = lambda i: jnp.maximum(i - 1, 0)
    const = lambda shape: pl.BlockSpec(shape, lambda i: (0,) * len(shape))
    resident = lambda shape: pl.BlockSpec(shape, lambda i: (0,) * len(shape),
                                          pipeline_mode=pl.Buffered(1))
    out_block = lambda w: pl.BlockSpec((tm, w), lambda i: (prev(i), 0))
    mod_block = lambda tile: pl.BlockSpec((1, mod3.shape[1], d),
                                          lambda i: (tile(i) // tiles_per_seq, 0, 0))
    kernel = functools.partial(_mix_kernel, tiles_per_seq=tiles_per_seq, wa=wa, wb=wb)
    return pl.pallas_call(
        kernel,
        grid=(n_tiles + 1,),
        in_specs=[pl.BlockSpec((tm, d), lambda i: (cur(i), 0)),
                  pl.BlockSpec((tm, d), lambda i: (prev(i), 0)),
                  mod_block(cur), mod_block(prev),
                  const((1, d)), const((1, d)), const((1, d)),
                  resident(w_in.shape),
                  const(conv_a_w.shape), const(conv_b_w.shape),
                  const((1, wb)), const((1, wb)), const((1, wb)),
                  resident(w_out.shape),
                  const(wr_cat.shape), const(wr_hi.shape), const((1, LANES))],
        out_specs=[out_block(d),
                   pl.BlockSpec((tm * slab_rows, LANES), lambda i: (prev(i), 0)),
                   out_block(LANES), out_block(LANES), out_block(LANES),
                   const((SUBLANES, LANES))],
        out_shape=[jax.ShapeDtypeStruct((n, d), jnp.float32),
                   jax.ShapeDtypeStruct((n * slab_rows, LANES), jnp.uint32),
                   jax.ShapeDtypeStruct((n, LANES), jnp.int32),
                   jax.ShapeDtypeStruct((n, LANES), jnp.float32),
                   jax.ShapeDtypeStruct((n, LANES), jnp.int32),
                   jax.ShapeDtypeStruct((SUBLANES, LANES), jnp.int32)],
        scratch_shapes=[pltpu.VMEM((wa // LANES, tm + HALO_A, LANES), jnp.float32),
                        pltpu.VMEM((wb // LANES, tm + HALO_B, LANES), jnp.float32),
                        pltpu.VMEM((tm, wb), jnp.float32),
                        pltpu.VMEM((tm, d), jnp.bfloat16),
                        pltpu.VMEM((1, LANES), jnp.float32)],
        compiler_params=_cparams(("arbitrary",)),
        name="mix",
    )(x2, x2, mod3, mod3, g_pre, g_post, g_ffn, w_in, conv_a_w, conv_b_w, conv_b_b, ln_g, ln_b,
      w_out, wr_cat, wr_hi, br)


def _plan_kernel(idx_ref, rank_ref, start_ref, dest_ref):
    tm = idx_ref.shape[0]
    lane = lax.broadcasted_iota(jnp.int32, (tm, LANES), 1)
    idx = idx_ref[...]
    rank = rank_ref[...]
    start = start_ref[...]
    out = jnp.zeros((tm, LANES), jnp.int32)
    for k in range(TOP_K):
        base = jnp.sum(jnp.where(lane == idx[:, k:k + 1], start, 0.0), axis=-1, keepdims=True)
        out = jnp.where(lane == k, base.astype(jnp.int32) + rank[:, k:k + 1], out)
    dest_ref[...] = out


def _plan(idx, rank, start_row):
    n = idx.shape[0]
    tm = min(PLAN_TM, n)
    assert n % tm == 0
    blk = pl.BlockSpec((tm, LANES), lambda i: (i, 0))
    return pl.pallas_call(
        _plan_kernel,
        grid=(n // tm,),
        in_specs=[blk, blk, pl.BlockSpec((1, LANES), lambda i: (0, 0))],
        out_specs=blk,
        out_shape=jax.ShapeDtypeStruct((n, LANES), jnp.int32),
        compiler_params=_cparams(("arbitrary",)),
        name="plan",
    )(idx, rank, start_row)


def _slab(ref, row, slab_rows, count=1):
    return ref.at[pl.ds(pl.multiple_of(row * slab_rows, slab_rows), count * slab_rows)]


def _dispatch_kernel(zstart_ref, tail_ref, dest_ref, hp_ref, xs_ref, zbuf, sem, zsem,
                     *, tm, n_experts, slab_rows):
    i = pl.program_id(0)

    def zero_copy(start):
        return pltpu.make_async_copy(zbuf, _slab(xs_ref, start, slab_rows, SUB), zsem)

    @pl.when(i == 0)
    def _():
        zbuf[...] = jnp.zeros_like(zbuf)
        for parity in range(2):
            for e in range(parity, n_experts, 2):
                zero_copy(zstart_ref[e]).start()
            for e in range(parity, n_experts, 2):
                zero_copy(zstart_ref[e]).wait()

        def tail_start(t, carry):
            zero_copy(tail_ref[0] + t * SUB).start()
            return carry

        def tail_wait(t, carry):
            zero_copy(tail_ref[0] + t * SUB).wait()
            return carry
        lax.fori_loop(0, tail_ref[1], tail_start, 0)
        lax.fori_loop(0, tail_ref[1], tail_wait, 0)

    def body(t, carry):
        for u in range(DISPATCH_UNROLL):
            tok = t * DISPATCH_UNROLL + u
            src = _slab(hp_ref, tok, slab_rows)
            for k in range(TOP_K):
                dst = _slab(xs_ref, dest_ref[0, 0, tok * TOP_K + k], slab_rows)
                pltpu.make_async_copy(src, dst, sem).start(priority=k % 2)
        return carry
    lax.fori_loop(0, tm // DISPATCH_UNROLL, body, 0)
    for k in range(TOP_K):
        pltpu.make_async_copy(hp_ref, _slab(xs_ref, 0, slab_rows, tm), sem).wait()


def _dispatch(hp, dest_flat, zstart, tail, p_rows, slab_rows):
    n = hp.shape[0] // slab_rows
    tm = min(DISPATCH_TM, n)
    assert n % tm == 0 and tm % DISPATCH_UNROLL == 0
    n_experts = zstart.shape[0]
    dest3 = dest_flat.reshape(n // tm, 1, tm * TOP_K)
    kernel = functools.partial(_dispatch_kernel, tm=tm, n_experts=n_experts, slab_rows=slab_rows)
    return pl.pallas_call(
        kernel,
        grid_spec=pltpu.PrefetchScalarGridSpec(
            num_scalar_prefetch=2,
            grid=(n // tm,),
            in_specs=[pl.BlockSpec((1, 1, tm * TOP_K), lambda i, *_: (i, 0, 0),
                                   memory_space=pltpu.SMEM),
                      pl.BlockSpec((tm * slab_rows, LANES), lambda i, *_: (i, 0))],
            out_specs=pl.BlockSpec(memory_space=pl.ANY),
            scratch_shapes=[pltpu.VMEM((SUB * slab_rows, LANES), jnp.uint32),
                            pltpu.SemaphoreType.DMA(()),
                            pltpu.SemaphoreType.DMA(())]),
        out_shape=jax.ShapeDtypeStruct((p_rows * slab_rows, LANES), jnp.uint32),
        compiler_params=_cparams(("arbitrary",)),
        name="dispatch",
    )(zstart, tail, dest3, hp)


def _expert_kernel(vexp_ref, vrow_ref, vnsub_ref, tail_ref,
                   xs_ref, wg_ref, wu_ref, wd_ref, bg_ref, bu_ref, bd_ref, ys_ref,
                   xb, acc, xstage, ostage, xsem, osem, zsem, *, n_f):
    del vexp_ref
    v = pl.program_id(0)
    j = pl.program_id(1)
    d = xb.shape[1]
    slab_rows = d // (2 * LANES)
    stage_rows = SUB * slab_rows
    f32, bf16 = jnp.float32, jnp.bfloat16
    nsub = vnsub_ref[v]
    row0 = vrow_ref[v]

    def stage_slot(ref, slot):
        return ref.at[pl.ds(pl.multiple_of(slot * stage_rows, stage_rows), stage_rows)]

    def x_copy(first_row, s):
        slot = s % X_SLOTS
        return pltpu.make_async_copy(_slab(xs_ref, first_row + s * SUB, slab_rows, SUB),
                                     stage_slot(xstage, slot), xsem.at[slot])

    def x_prefetch(first_row, count):
        for s in range(X_AHEAD):
            @pl.when(s < count)
            def _():
                x_copy(first_row, s).start()

    def o_copy(s, slot):
        return pltpu.make_async_copy(stage_slot(ostage, slot),
                                     _slab(ys_ref, row0 + s * SUB, slab_rows, SUB), osem.at[slot])

    @pl.when((v == 0) & (j == 0))
    def _():
        ostage[0:stage_rows, :] = jnp.zeros((stage_rows, LANES), jnp.uint32)

        def z_copy(t):
            return pltpu.make_async_copy(stage_slot(ostage, 0),
                                         _slab(ys_ref, tail_ref[0] + t * SUB, slab_rows, SUB), zsem)

        def z_start(t, carry):
            z_copy(t).start()
            return carry

        def z_wait(t, carry):
            z_copy(t).wait()
            return carry
        lax.fori_loop(0, tail_ref[1], z_start, 0)
        lax.fori_loop(0, tail_ref[1], z_wait, 0)
        x_prefetch(row0, nsub)

    @pl.when(j == 0)
    def _():
        def body(s, carry):
            slot = s % X_SLOTS

            @pl.when(s + X_AHEAD < nsub)
            def _():
                x_copy(row0, s + X_AHEAD).start()
            x_copy(row0, s).wait()
            base = pl.multiple_of(s * SUB, SUB)
            for jj in range(slab_rows):
                lo, hi = _unpack_slab_words(
                    xstage[pl.ds(slot * stage_rows + jj, SUB, stride=slab_rows), :])
                xb[pl.ds(base, SUB), jj * LANES:(jj + 1) * LANES] = lo.astype(bf16)
                xb[pl.ds(base, SUB), d // 2 + jj * LANES:d // 2 + (jj + 1) * LANES] = hi.astype(bf16)
            acc[pl.ds(base, SUB), :] = jnp.zeros((SUB, d), f32)
            return carry
        lax.fori_loop(0, nsub, body, 0)

    def chain(base, m):
        wg = wg_ref[...].astype(bf16)
        wu = wu_ref[...].astype(bf16)
        wd = wd_ref[...].astype(bf16)
        piece = min(m, CHAIN_PIECE * SUB)
        for p in range(m // piece):
            rows = pl.ds(pl.multiple_of(base + p * piece, piece), piece)
            xt = xb[rows, :]
            g = jnp.dot(xt, wg, preferred_element_type=f32) + bg_ref[...]
            u = jnp.dot(xt, wu, preferred_element_type=f32) + bu_ref[...]
            g = jnp.minimum(g, SWIGLU_LIMIT)
            u = jnp.clip(u, -SWIGLU_LIMIT, SWIGLU_LIMIT)
            a = (u + 1.0) * (g * jax.nn.sigmoid(SWIGLU_ALPHA * g))
            acc[rows, :] += jnp.dot(a.astype(bf16), wd, preferred_element_type=f32)

    def big_body(q, carry):
        chain(pl.multiple_of(q * (CHAIN_MAX * SUB), CHAIN_MAX * SUB), CHAIN_MAX * SUB)
        return carry
    lax.fori_loop(0, nsub // CHAIN_MAX, big_body, 0)
    units = CHAIN_MAX // 2
    while units >= 1:
        @pl.when((nsub & units) != 0)
        def _(units=units):
            start = (nsub // (2 * units)) * (2 * units)
            chain(pl.multiple_of(start * SUB, units * SUB), units * SUB)
        units //= 2

    @pl.when((j == n_f - 1) & (v + 1 < pl.num_programs(0)))
    def _():
        nxt = jnp.minimum(v + 1, pl.num_programs(0) - 1)
        x_prefetch(vrow_ref[nxt], vnsub_ref[nxt])

    @pl.when((j == n_f - 1) & (nsub > 0))
    def _():
        def body(s, carry):
            slot = s % O_SLOTS

            @pl.when(s >= O_SLOTS)
            def _():
                o_copy(s - O_SLOTS, slot).wait()
            words = _pack_slab_words(acc[pl.ds(pl.multiple_of(s * SUB, SUB), SUB), :] + bd_ref[...])
            for jj in range(slab_rows):
                ostage[pl.ds(slot * stage_rows + jj, SUB, stride=slab_rows), :] = (
                    words[:, jj * LANES:(jj + 1) * LANES])
            o_copy(s, slot).start()
            return carry
        lax.fori_loop(0, nsub, body, 0)

        def drain(s, carry):
            o_copy(s, s % O_SLOTS).wait()
            return carry
        lax.fori_loop(jnp.maximum(nsub - O_SLOTS, 0), nsub, drain, 0)


def _expert(xs, w_gate, b_gate, w_up, b_up, w_down, b_down, vexp, vrow, vnsub, tail, p_rows):
    n_experts, d, f = w_gate.shape
    tf = min(EXPERT_TF, f)
    assert f % tf == 0 and EXPERT_ROWS % SUB == 0 and EXPERT_ROWS // SUB < 2 * CHAIN_MAX
    n_f = f // tf
    slab_rows = d // (2 * LANES)
    n_visits = vexp.shape[0]

    def w_idx(v, j, ve, vr, vn, tl):
        return jnp.where(vn[v] == 0, n_f - 1, j)

    kernel = functools.partial(_expert_kernel, n_f=n_f)
    return pl.pallas_call(
        kernel,
        grid_spec=pltpu.PrefetchScalarGridSpec(
            num_scalar_prefetch=4,
            grid=(n_visits, n_f),
            in_specs=[
                pl.BlockSpec(memory_space=pl.ANY),
                pl.BlockSpec((None, d, tf), lambda v, j, ve, *s: (ve[v], 0, w_idx(v, j, ve, *s))),
                pl.BlockSpec((None, d, tf), lambda v, j, ve, *s: (ve[v], 0, w_idx(v, j, ve, *s))),
                pl.BlockSpec((None, tf, d), lambda v, j, ve, *s: (ve[v], w_idx(v, j, ve, *s), 0)),
                pl.BlockSpec((None, 1, tf), lambda v, j, ve, *s: (ve[v], 0, w_idx(v, j, ve, *s))),
                pl.BlockSpec((None, 1, tf), lambda v, j, ve, *s: (ve[v], 0, w_idx(v, j, ve, *s))),
                pl.BlockSpec((None, 1, d), lambda v, j, ve, *s: (ve[v], 0, 0)),
            ],
            out_specs=pl.BlockSpec(memory_space=pl.ANY),
            scratch_shapes=[pltpu.VMEM((EXPERT_ROWS, d), jnp.bfloat16),
                            pltpu.VMEM((EXPERT_ROWS, d), jnp.float32),
                            pltpu.VMEM((X_SLOTS * SUB * slab_rows, LANES), jnp.uint32),
                            pltpu.VMEM((O_SLOTS * SUB * slab_rows, LANES), jnp.uint32),
                            pltpu.SemaphoreType.DMA((X_SLOTS,)),
                            pltpu.SemaphoreType.DMA((O_SLOTS,)),
                            pltpu.SemaphoreType.DMA(())]),
        out_shape=jax.ShapeDtypeStruct((p_rows * slab_rows, LANES), jnp.uint32),
        compiler_params=_cparams(("arbitrary", "arbitrary")),
        name="expert",
    )(vexp, vrow, vnsub, tail, xs, w_gate, w_up, w_down,
      b_gate.reshape(n_experts, 1, f), b_up.reshape(n_experts, 1, f),
      b_down.reshape(n_experts, 1, d))


def _combine_kernel(dest_ref, dnext_ref, ys_ref, wgt_ref, x1_ref, mod_ref, g_ref, o_ref,
                    buf, ybuf, sem, *, tm):
    i = pl.program_id(0)
    d = x1_ref.shape[1]
    slab_rows = d // (2 * LANES)
    half = TOP_K * tm

    def gather(idx_ref, slot):
        def body(t, carry):
            for u in range(COMBINE_UNROLL):
                tok = t * COMBINE_UNROLL + u
                for k in range(TOP_K):
                    src = _slab(ys_ref, idx_ref[0, 0, tok * TOP_K + k], slab_rows)
                    dst = _slab(buf, slot * half + k * tm + tok, slab_rows)
                    pltpu.make_async_copy(src, dst, sem.at[slot]).start(priority=k % 2)
            return carry
        lax.fori_loop(0, tm // COMBINE_UNROLL, body, 0)

    slot = i % 2

    @pl.when(i == 0)
    def _():
        gather(dest_ref, 0)

    @pl.when(i + 1 < pl.num_programs(0))
    def _():
        gather(dnext_ref, 1 - slot)

    pltpu.make_async_copy(_slab(ys_ref, 0, slab_rows, half), _slab(buf, slot * half, slab_rows, half),
                          sem.at[slot]).wait()

    wgt = wgt_ref[...]
    base = slot * half * slab_rows
    for jj in range(slab_rows):
        ylo = jnp.zeros((tm, LANES), jnp.float32)
        yhi = jnp.zeros((tm, LANES), jnp.float32)
        for k in range(TOP_K):
            lo, hi = _unpack_slab_words(
                buf[pl.ds(base + k * tm * slab_rows + jj, tm, stride=slab_rows), :])
            ylo = ylo + wgt[:, k:k + 1] * lo
            yhi = yhi + wgt[:, k:k + 1] * hi
        ybuf[:, jj * LANES:(jj + 1) * LANES] = ylo
        ybuf[:, d // 2 + jj * LANES:d // 2 + (jj + 1) * LANES] = yhi
    mod = mod_ref[0]
    o_ref[...] = x1_ref[...] + mod[5:6] * _rms(ybuf[...], g_ref[...])


def _combine(ys, dest_flat, wgt, x1, mod3, seq, g_post):
    n, d = x1.shape
    slab_rows = d // (2 * LANES)
    tm = COMBINE_TM
    assert seq % tm == 0
    tiles_per_seq = seq // tm
    n_tiles = n // tm
    dest3 = dest_flat.reshape(n_tiles, 1, tm * TOP_K)
    kernel = functools.partial(_combine_kernel, tm=tm)
    return pl.pallas_call(
        kernel,
        grid=(n_tiles,),
        in_specs=[pl.BlockSpec((1, 1, tm * TOP_K), lambda i: (i, 0, 0), memory_space=pltpu.SMEM),
                  pl.BlockSpec((1, 1, tm * TOP_K), lambda i: (jnp.minimum(i + 1, n_tiles - 1), 0, 0),
                               memory_space=pltpu.SMEM),
                  pl.BlockSpec(memory_space=pl.ANY),
                  pl.BlockSpec((tm, LANES), lambda i: (i, 0)),
                  pl.BlockSpec((tm, d), lambda i: (i, 0)),
                  pl.BlockSpec((1, mod3.shape[1], d), lambda i: (i // tiles_per_seq, 0, 0)),
                  pl.BlockSpec((1, d), lambda i: (0, 0))],
        out_specs=pl.BlockSpec((tm, d), lambda i: (i, 0)),
        out_shape=jax.ShapeDtypeStruct((n, d), jnp.float32),
        scratch_shapes=[pltpu.VMEM((2 * TOP_K * tm * slab_rows, LANES), jnp.uint32),
                        pltpu.VMEM((tm, d), jnp.float32),
                        pltpu.SemaphoreType.DMA((2,))],
        compiler_params=_cparams(("arbitrary",)),
        name="combine",
    )(dest3, dest3, ys, wgt, x1, mod3, g_post)


def _routing_tables(counts, n_assign):
    n_experts = counts.shape[0]
    p_rows = n_assign + n_experts * SUB
    n_visits = n_experts + -(-p_rows // EXPERT_ROWS)

    padded = jnp.maximum((counts + SUB - 1) // SUB, 1) * SUB
    e_ids = jnp.arange(n_experts, dtype=jnp.int32)
    lower = e_ids[None, :] <= e_ids[:, None]
    pad_end = jnp.sum(jnp.where(lower, padded[None, :], 0), axis=1)
    pad_start = pad_end - padded
    total = pad_end[-1]
    zstart = (pad_start + counts).astype(jnp.int32)
    tail_xs = jnp.stack([total, (p_rows + SUB - total) // SUB]).astype(jnp.int32)
    tail_ys = jnp.stack([total, (p_rows - total) // SUB]).astype(jnp.int32)

    n_chunk = (padded + EXPERT_ROWS - 1) // EXPERT_ROWS
    chunk_end = jnp.sum(jnp.where(lower, n_chunk[None, :], 0), axis=1)
    visit = jnp.arange(n_visits, dtype=jnp.int32)
    used = visit < chunk_end[-1]
    vexp = jnp.minimum(jnp.sum((chunk_end[None, :] <= visit[:, None]).astype(jnp.int32), axis=1),
                       n_experts - 1)
    onehot = vexp[:, None] == e_ids[None, :]
    pick = lambda a: jnp.sum(jnp.where(onehot, a[None, :], 0), axis=1)
    chunk = visit - pick(chunk_end - n_chunk)
    vrow = jnp.where(used, pick(pad_start) + chunk * EXPERT_ROWS, 0)
    vnsub = jnp.where(used, jnp.minimum(pick(padded) - chunk * EXPERT_ROWS, EXPERT_ROWS) // SUB, 0)
    return (pad_start.astype(jnp.int32), zstart, tail_xs, tail_ys, vexp.astype(jnp.int32),
            vrow.astype(jnp.int32), vnsub.astype(jnp.int32), p_rows)


def kernel(x, c, w_ada, b_ada, g_pre_mix, g_post_mix, w_in, conv_a_w, conv_b_w, conv_b_b, ln_b_g, ln_b_b, w_out, g_pre_ffn, g_post_ffn, w_router, b_router, w_gate, b_gate, w_up, b_up, w_down, b_down):
    bsz, seq, d = x.shape
    n = bsz * seq
    depth = w_ada.shape[0]
    n_experts = w_router.shape[-1]
    assert n_experts <= LANES
    xf = x.reshape(n, d)
    for l in range(depth):
        mod = _ada(c, w_ada[l], b_ada[l])
        n_mod = mod.shape[1] // d
        mod3 = mod.reshape(bsz, n_mod, d)

        wr = jnp.zeros((d, LANES), jnp.float32).at[:, :n_experts].set(w_router[l])
        wr_hi = wr.astype(jnp.bfloat16)
        wr_lo = (wr - wr_hi.astype(jnp.float32)).astype(jnp.bfloat16)
        br = jnp.full((1, LANES), NEG_BIG, jnp.float32).at[0, :n_experts].set(b_router[l])

        x1, hp, idx, wgt, rank, cnt = _mix(
            xf, mod3, seq, g_pre_mix[l].reshape(1, d), g_post_mix[l].reshape(1, d),
            g_pre_ffn[l].reshape(1, d), w_in[l].astype(jnp.bfloat16), conv_a_w[l], conv_b_w[l],
            conv_b_b[l].reshape(1, -1), ln_b_g[l].reshape(1, -1), ln_b_b[l].reshape(1, -1),
            w_out[l].astype(jnp.bfloat16), jnp.concatenate([wr_hi, wr_lo], axis=1), wr_hi, br)

        (pad_start, zstart, tail_xs, tail_ys, vexp, vrow, vnsub,
         p_rows) = _routing_tables(cnt[0, :n_experts], n * TOP_K)
        start_row = jnp.zeros((1, LANES), jnp.float32).at[0, :n_experts].set(
            pad_start.astype(jnp.float32))
        dest = _plan(idx, rank, start_row)
        dest_flat = dest[:, :TOP_K].reshape(n * TOP_K)

        slab_rows = d // (2 * LANES)
        xs = _dispatch(hp, dest_flat, zstart, tail_xs, p_rows + SUB, slab_rows)
        ys = _expert(xs, w_gate[l], b_gate[l], w_up[l], b_up[l], w_down[l], b_down[l],
                     vexp, vrow, vnsub, tail_ys, p_rows)
        xf = _combine(ys, dest_flat, wgt, x1, mod3, seq, g_post_ffn[l].reshape(1, d))
    return xf.reshape(bsz, seq, d)
```

```python
import functools

import jax
import jax.numpy as jnp
from jax import lax
from jax.experimental import pallas as pl
from jax.experimental.pallas import tpu as pltpu

EPS = 1e-6
TOP_K = 4
CONV_A = 3
CONV_B = 31
SWIGLU_ALPHA = 1.702
SWIGLU_LIMIT = 7.0

LANES = 128
SUBLANES = 8
VMEM_LIMIT_BYTES = 62 * 1024 * 1024

ADA_TN = 1024
MIX_TM = 256
HALO_A = 8
HALO_B = 32
PLAN_TM = 2048
DISPATCH_TM = 1024
DISPATCH_UNROLL = 8
SUB = 128
EXPERT_ROWS = 2176
EXPERT_TF = 512
CHAIN_MAX = 8
CHAIN_PIECE = 8
X_AHEAD = 3
X_SLOTS = X_AHEAD + 1
O_SLOTS = 3
COMBINE_TM = 256
COMBINE_UNROLL = 8
NEG_BIG = -1e30


def _cparams(sem):
    return pltpu.CompilerParams(dimension_semantics=sem, vmem_limit_bytes=VMEM_LIMIT_BYTES)


def _rms(x, g):
    return x * lax.rsqrt(jnp.mean(x * x, axis=-1, keepdims=True) + EPS) * g


def _pack_slab_words(v):
    half = v.shape[1] // 2
    lo = pltpu.bitcast(v[:, :half].astype(jnp.bfloat16).astype(jnp.float32), jnp.uint32)
    hi = pltpu.bitcast(v[:, half:].astype(jnp.bfloat16).astype(jnp.float32), jnp.uint32)
    return (lo >> 16) | (hi & jnp.uint32(0xFFFF0000))


def _unpack_slab_words(w):
    lo = pltpu.bitcast(w << 16, jnp.float32)
    hi = pltpu.bitcast(w & jnp.uint32(0xFFFF0000), jnp.float32)
    return lo, hi


def _ada_kernel(c_ref, w_ref, b_ref, o_ref):
    c = c_ref[...]
    ca = (c * jax.nn.sigmoid(c)).astype(jnp.bfloat16)
    o_ref[...] = jnp.dot(ca, w_ref[...].astype(jnp.bfloat16),
                         preferred_element_type=jnp.float32) + b_ref[...]


def _ada(c, w_ada, b_ada):
    bsz, d = c.shape
    n_out = w_ada.shape[1]
    tn = min(ADA_TN, n_out)
    assert n_out % tn == 0 and bsz <= SUBLANES
    c8 = jnp.zeros((SUBLANES, d), jnp.float32).at[:bsz].set(c)
    out = pl.pallas_call(
        _ada_kernel,
        grid=(n_out // tn,),
        in_specs=[pl.BlockSpec((SUBLANES, d), lambda j: (0, 0)),
                  pl.BlockSpec((d, tn), lambda j: (0, j)),
                  pl.BlockSpec((1, tn), lambda j: (0, j))],
        out_specs=pl.BlockSpec((SUBLANES, tn), lambda j: (0, j)),
        out_shape=jax.ShapeDtypeStruct((SUBLANES, n_out), jnp.float32),
        compiler_params=_cparams(("arbitrary",)),
        name="ada",
    )(c8, w_ada, b_ada.reshape(1, n_out))
    return out[:bsz]


def _mix_kernel(x_ref, xp_ref, mod_ref, modp_ref, gpre_ref, gpost_ref, gffn_ref, win_ref,
                caw_ref, cbw_ref, cbb_ref, lng_ref, lnb_ref, wout_ref, wrc_ref, wrh_ref, br_ref,
                x1_ref, hp_ref, idx_ref, wgt_ref, rank_ref, cnt_ref,
                pa_buf, u_buf, v_buf, ycat, run_cnt, *, tiles_per_seq, wa, wb):
    i = pl.program_id(0)
    tm, d = x_ref.shape
    f32, bf16 = jnp.float32, jnp.bfloat16

    @pl.when(i == 0)
    def _():
        run_cnt[...] = jnp.zeros_like(run_cnt)
        ycat[...] = jnp.zeros_like(ycat)

    @pl.when(i % tiles_per_seq == 0)
    def _():
        pa_buf[:, 0:HALO_A, :] = jnp.zeros((wa // LANES, HALO_A, LANES), f32)
        u_buf[:, 0:HALO_B, :] = jnp.zeros((wb // LANES, HALO_B, LANES), f32)

    y_prev = jnp.dot(ycat[...], wout_ref[...], preferred_element_type=f32)

    x = x_ref[...]
    mod = mod_ref[0]
    h = (_rms(x, gpre_ref[...]) * (1.0 + mod[1:2]) + mod[0:1]).astype(bf16)
    proj_b = jnp.dot(h, win_ref[:, 3 * wa:3 * wa + 2 * wb], preferred_element_type=f32)
    proj_a = jnp.dot(h, win_ref[:, 0:3 * wa], preferred_element_type=f32)

    for cb in range(wa // LANES):
        sl = slice(cb * LANES, (cb + 1) * LANES)
        pa = proj_a[:, sl] * proj_a[:, 2 * wa + cb * LANES:2 * wa + (cb + 1) * LANES]
        pa_buf[cb, HALO_A:HALO_A + tm, :] = pa
        conv = caw_ref[CONV_A - 1:CONV_A, sl] * pa
        for k in range(CONV_A - 1):
            off = HALO_A - (CONV_A - 1) + k
            conv = conv + caw_ref[k:k + 1, sl] * pa_buf[cb, off:off + tm, :]
        ycat[:, sl] = (proj_a[:, wa + cb * LANES:wa + (cb + 1) * LANES] * conv).astype(bf16)
        pa_buf[cb, 0:HALO_A, :] = pa_buf[cb, tm:tm + HALO_A, :]

    for cb in range(wb // LANES):
        sl = slice(cb * LANES, (cb + 1) * LANES)
        u = proj_b[:, sl] * jax.nn.sigmoid(proj_b[:, wb + cb * LANES:wb + (cb + 1) * LANES])
        u_buf[cb, HALO_B:HALO_B + tm, :] = u
        acc = cbb_ref[:, sl] + cbw_ref[CONV_B - 1:CONV_B, sl] * u
        for k in range(CONV_B - 1):
            off = HALO_B - (CONV_B - 1) + k
            acc = acc + cbw_ref[k:k + 1, sl] * u_buf[cb, off:off + tm, :]
        v_buf[:, sl] = acc
        u_buf[cb, 0:HALO_B, :] = u_buf[cb, tm:tm + HALO_B, :]
    v = v_buf[...]
    mu = jnp.mean(v, axis=-1, keepdims=True)
    vc = v - mu
    var = jnp.mean(vc * vc, axis=-1, keepdims=True)
    yb = vc * lax.rsqrt(var + EPS) * lng_ref[...] + lnb_ref[...]
    ycat[:, wa:wa + wb] = (yb * jax.nn.sigmoid(yb)).astype(bf16)

    _mix_stage2(i >= 1, y_prev, xp_ref, modp_ref, gpost_ref, gffn_ref, wrc_ref, wrh_ref, br_ref,
                x1_ref, hp_ref, idx_ref, wgt_ref, rank_ref, cnt_ref, run_cnt)


def _mix_stage2(valid, y, x_ref, mod_ref, gpost_ref, gffn_ref, wrc_ref, wrh_ref, br_ref,
                x1_ref, hp_ref, idx_ref, wgt_ref, rank_ref, cnt_ref, run_cnt):
    tm, d = x_ref.shape
    f32, bf16 = jnp.float32, jnp.bfloat16
    x = x_ref[...]
    mod = mod_ref[0]
    x1 = x + mod[2:3] * _rms(y, gpost_ref[...])
    x1_ref[...] = x1

    h2 = _rms(x1, gffn_ref[...]) * (1.0 + mod[4:5]) + mod[3:4]
    words = _pack_slab_words(h2)
    slab_rows = d // (2 * LANES)
    for j in range(slab_rows):
        hp_ref[pl.ds(j, tm, stride=slab_rows), :] = words[:, j * LANES:(j + 1) * LANES]

    h2_hi = h2.astype(bf16)
    h2_lo = (h2 - h2_hi.astype(f32)).astype(bf16)
    both = jnp.dot(h2_hi, wrc_ref[...], preferred_element_type=f32)
    logits = (both[:, :LANES] + both[:, LANES:]
              + jnp.dot(h2_lo, wrh_ref[...], preferred_element_type=f32)
              + br_ref[...])

    lane = lax.broadcasted_iota(jnp.int32, (tm, LANES), 1)
    lane_f = lane.astype(f32)
    vals, idxs = [], []
    cur = logits
    for _ in range(TOP_K):
        m = jnp.max(cur, axis=-1, keepdims=True)
        ix = jnp.min(jnp.where(cur == m, lane_f, float(LANES)), axis=-1,
                     keepdims=True).astype(jnp.int32)
        vals.append(m)
        idxs.append(ix)
        cur = jnp.where(lane == ix, -jnp.inf, cur)
    exps = [jnp.exp(vk - vals[0]) for vk in vals]
    denom = exps[0]
    for ek in exps[1:]:
        denom = denom + ek
    inv = 1.0 / denom

    row = lax.broadcasted_iota(jnp.int32, (tm, tm), 0)
    col = lax.broadcasted_iota(jnp.int32, (tm, tm), 1)
    tri = jnp.where(col < row, 1.0, 0.0).astype(bf16)
    run = run_cnt[...]
    idx_out = jnp.zeros((tm, LANES), jnp.int32)
    wgt_out = jnp.zeros((tm, LANES), f32)
    rank_out = jnp.zeros((tm, LANES), f32)
    onehots = [jnp.where(lane == idxs[k], 1.0, 0.0) for k in range(TOP_K)]
    before_all = jnp.dot(tri, jnp.concatenate(onehots, axis=1).astype(bf16),
                         preferred_element_type=f32)
    for k in range(TOP_K):
        oh = onehots[k]
        before = before_all[:, k * LANES:(k + 1) * LANES]
        rank_k = jnp.sum(oh * (before + run), axis=-1, keepdims=True)
        run = run + jnp.sum(oh, axis=0, keepdims=True)
        idx_out = jnp.where(lane == k, idxs[k], idx_out)
        wgt_out = jnp.where(lane == k, exps[k] * inv, wgt_out)
        rank_out = jnp.where(lane == k, rank_k, rank_out)
    run = jnp.where(valid, run, run_cnt[...])
    run_cnt[...] = run
    idx_ref[...] = idx_out
    wgt_ref[...] = wgt_out
    rank_ref[...] = rank_out.astype(jnp.int32)
    cnt_ref[...] = jnp.broadcast_to(run, cnt_ref.shape).astype(jnp.int32)


def _mix(x2, mod3, seq, g_pre, g_post, g_ffn, w_in, conv_a_w, conv_b_w, conv_b_b, ln_g, ln_b,
         w_out, wr_cat, wr_hi, br):
    n, d = x2.shape
    wa = conv_a_w.shape[1]
    wb = conv_b_w.shape[1]
    tm = MIX_TM
    assert seq % tm == 0 and tm >= HALO_B and wa % LANES == 0 and wb % LANES == 0
    assert wa + wb == d and d % (2 * LANES) == 0
    tiles_per_seq = seq // tm
    n_tiles = n // tm
    slab_rows = d // (2 * LANES)
    cur = lambda i: jnp.minimum(i, n_tiles - 1)
    prev = lambda i: jnp.maximum(i - 1, 0)
    const = lambda shape: pl.BlockSpec(shape, lambda i: (0,) * len(shape))
    resident = lambda shape: pl.BlockSpec(shape, lambda i: (0,) * len(shape),
                                          pipeline_mode=pl.Buffered(1))
    out_block = lambda w: pl.BlockSpec((tm, w), lambda i: (prev(i), 0))
    mod_block = lambda tile: pl.BlockSpec((1, mod3.shape[1], d),
                                          lambda i: (tile(i) // tiles_per_seq, 0, 0))
    kernel = functools.partial(_mix_kernel, tiles_per_seq=tiles_per_seq, wa=wa, wb=wb)
    return pl.pallas_call(
        kernel,
        grid=(n_tiles + 1,),
        in_specs=[pl.BlockSpec((tm, d), lambda i: (cur(i), 0)),
                  pl.BlockSpec((tm, d), lambda i: (prev(i), 0)),
                  mod_block(cur), mod_block(prev),
                  const((1, d)), const((1, d)), const((1, d)),
                  resident(w_in.shape),
                  const(conv_a_w.shape), const(conv_b_w.shape),
                  const((1, wb)), const((1, wb)), const((1, wb)),
                  resident(w_out.shape),
                  const(wr_cat.shape), const(wr_hi.shape), const((1, LANES))],
        out_specs=[out_block(d),
                   pl.BlockSpec((tm * slab_rows, LANES), lambda i: (prev(i), 0)),
                   out_block(LANES), out_block(LANES), out_block(LANES),
                   const((SUBLANES, LANES))],
        out_shape=[jax.ShapeDtypeStruct((n, d), jnp.float32),
                   jax.ShapeDtypeStruct((n * slab_rows, LANES), jnp.uint32),
                   jax.ShapeDtypeStruct((n, LANES), jnp.int32),
                   jax.ShapeDtypeStruct((n, LANES), jnp.float32),
                   jax.ShapeDtypeStruct((n, LANES), jnp.int32),
                   jax.ShapeDtypeStruct((SUBLANES, LANES), jnp.int32)],
        scratch_shapes=[pltpu.VMEM((wa // LANES, tm + HALO_A, LANES), jnp.float32),
                        pltpu.VMEM((wb // LANES, tm + HALO_B, LANES), jnp.float32),
                        pltpu.VMEM((tm, wb), jnp.float32),
                        pltpu.VMEM((tm, d), jnp.bfloat16),
                        pltpu.VMEM((1, LANES), jnp.float32)],
        compiler_params=_cparams(("arbitrary",)),
        name="mix",
    )(x2, x2, mod3, mod3, g_pre, g_post, g_ffn, w_in, conv_a_w, conv_b_w, conv_b_b, ln_g, ln_b,
      w_out, wr_cat, wr_hi, br)


def _plan_kernel(idx_ref, rank_ref, start_ref, dest_ref):
    tm = idx_ref.shape[0]
    lane = lax.broadcasted_iota(jnp.int32, (tm, LANES), 1)
    idx = idx_ref[...]
    rank = rank_ref[...]
    start = start_ref[...]
    out = jnp.zeros((tm, LANES), jnp.int32)
    for k in range(TOP_K):
        base = jnp.sum(jnp.where(lane == idx[:, k:k + 1], start, 0.0), axis=-1, keepdims=True)
        out = jnp.where(lane == k, base.astype(jnp.int32) + rank[:, k:k + 1], out)
    dest_ref[...] = out


def _plan(idx, rank, start_row):
    n = idx.shape[0]
    tm = min(PLAN_TM, n)
    assert n % tm == 0
    blk = pl.BlockSpec((tm, LANES), lambda i: (i, 0))
    return pl.pallas_call(
        _plan_kernel,
        grid=(n // tm,),
        in_specs=[blk, blk, pl.BlockSpec((1, LANES), lambda i: (0, 0))],
        out_specs=blk,
        out_shape=jax.ShapeDtypeStruct((n, LANES), jnp.int32),
        compiler_params=_cparams(("arbitrary",)),
        name="plan",
    )(idx, rank, start_row)


def _slab(ref, row, slab_rows, count=1):
    return ref.at[pl.ds(pl.multiple_of(row * slab_rows, slab_rows), count * slab_rows)]


def _dispatch_kernel(zstart_ref, tail_ref, dest_ref, hp_ref, xs_ref, zbuf, sem, zsem,
                     *, tm, n_experts, slab_rows):
    i = pl.program_id(0)

    def zero_copy(start):
        return pltpu.make_async_copy(zbuf, _slab(xs_ref, start, slab_rows, SUB), zsem)

    @pl.when(i == 0)
    def _():
        zbuf[...] = jnp.zeros_like(zbuf)
        for parity in range(2):
            for e in range(parity, n_experts, 2):
                zero_copy(zstart_ref[e]).start()
            for e in range(parity, n_experts, 2):
                zero_copy(zstart_ref[e]).wait()

        def tail_start(t, carry):
            zero_copy(tail_ref[0] + t * SUB).start()
            return carry

        def tail_wait(t, carry):
            zero_copy(tail_ref[0] + t * SUB).wait()
            return carry
        lax.fori_loop(0, tail_ref[1], tail_start, 0)
        lax.fori_loop(0, tail_ref[1], tail_wait, 0)

    def body(t, carry):
        for u in range(DISPATCH_UNROLL):
            tok = t * DISPATCH_UNROLL + u
            src = _slab(hp_ref, tok, slab_rows)
            for k in range(TOP_K):
                dst = _slab(xs_ref, dest_ref[0, 0, tok * TOP_K + k], slab_rows)
                pltpu.make_async_copy(src, dst, sem).start(priority=k % 2)
        return carry
    lax.fori_loop(0, tm // DISPATCH_UNROLL, body, 0)
    for k in range(TOP_K):
        pltpu.make_async_copy(hp_ref, _slab(xs_ref, 0, slab_rows, tm), sem).wait()


def _dispatch(hp, dest_flat, zstart, tail, p_rows, slab_rows):
    n = hp.shape[0] // slab_rows
    tm = min(DISPATCH_TM, n)
    assert n % tm == 0 and tm % DISPATCH_UNROLL == 0
    n_experts = zstart.shape[0]
    dest3 = dest_flat.reshape(n // tm, 1, tm * TOP_K)
    kernel = functools.partial(_dispatch_kernel, tm=tm, n_experts=n_experts, slab_rows=slab_rows)
    return pl.pallas_call(
        kernel,
        grid_spec=pltpu.PrefetchScalarGridSpec(
            num_scalar_prefetch=2,
            grid=(n // tm,),
            in_specs=[pl.BlockSpec((1, 1, tm * TOP_K), lambda i, *_: (i, 0, 0),
                                   memory_space=pltpu.SMEM),
                      pl.BlockSpec((tm * slab_rows, LANES), lambda i, *_: (i, 0))],
            out_specs=pl.BlockSpec(memory_space=pl.ANY),
            scratch_shapes=[pltpu.VMEM((SUB * slab_rows, LANES), jnp.uint32),
                            pltpu.SemaphoreType.DMA(()),
                            pltpu.SemaphoreType.DMA(())]),
        out_shape=jax.ShapeDtypeStruct((p_rows * slab_rows, LANES), jnp.uint32),
        compiler_params=_cparams(("arbitrary",)),
        name="dispatch",
    )(zstart, tail, dest3, hp)


def _expert_kernel(vexp_ref, vrow_ref, vnsub_ref, tail_ref,
                   xs_ref, wg_ref, wu_ref, wd_ref, bg_ref, bu_ref, bd_ref, ys_ref,
                   xb, acc, xstage, ostage, xsem, osem, zsem, *, n_f):
    del vexp_ref
    v = pl.program_id(0)
    j = pl.program_id(1)
    d = xb.shape[1]
    slab_rows = d // (2 * LANES)
    stage_rows = SUB * slab_rows
    f32, bf16 = jnp.float32, jnp.bfloat16
    nsub = vnsub_ref[v]
    row0 = vrow_ref[v]

    def stage_slot(ref, slot):
        return ref.at[pl.ds(pl.multiple_of(slot * stage_rows, stage_rows), stage_rows)]

    def x_copy(first_row, s):
        slot = s % X_SLOTS
        return pltpu.make_async_copy(_slab(xs_ref, first_row + s * SUB, slab_rows, SUB),
                                     stage_slot(xstage, slot), xsem.at[slot])

    def x_prefetch(first_row, count):
        for s in range(X_AHEAD):
            @pl.when(s < count)
            def _():
                x_copy(first_row, s).start()

    def o_copy(s, slot):
        return pltpu.make_async_copy(stage_slot(ostage, slot),
                                     _slab(ys_ref, row0 + s * SUB, slab_rows, SUB), osem.at[slot])

    @pl.when((v == 0) & (j == 0))
    def _():
        ostage[0:stage_rows, :] = jnp.zeros((stage_rows, LANES), jnp.uint32)

        def z_copy(t):
            return pltpu.make_async_copy(stage_slot(ostage, 0),
                                         _slab(ys_ref, tail_ref[0] + t * SUB, slab_rows, SUB), zsem)

        def z_start(t, carry):
            z_copy(t).start()
            return carry

        def z_wait(t, carry):
            z_copy(t).wait()
            return carry
        lax.fori_loop(0, tail_ref[1], z_start, 0)
        lax.fori_loop(0, tail_ref[1], z_wait, 0)
        x_prefetch(row0, nsub)

    @pl.when(j == 0)
    def _():
        def body(s, carry):
            slot = s % X_SLOTS

            @pl.when(s + X_AHEAD < nsub)
            def _():
                x_copy(row0, s + X_AHEAD).start()
            x_copy(row0, s).wait()
            base = pl.multiple_of(s * SUB, SUB)
            for jj in range(slab_rows):
                lo, hi = _unpack_slab_words(
                    xstage[pl.ds(slot * stage_rows + jj, SUB, stride=slab_rows), :])
                xb[pl.ds(base, SUB), jj * LANES:(jj + 1) * LANES] = lo.astype(bf16)
                xb[pl.ds(base, SUB), d // 2 + jj * LANES:d // 2 + (jj + 1) * LANES] = hi.astype(bf16)
            acc[pl.ds(base, SUB), :] = jnp.zeros((SUB, d), f32)
            return carry
        lax.fori_loop(0, nsub, body, 0)

    def chain(base, m):
        wg = wg_ref[...].astype(bf16)
        wu = wu_ref[...].astype(bf16)
        wd = wd_ref[...].astype(bf16)
        piece = min(m, CHAIN_PIECE * SUB)
        for p in range(m // piece):
            rows = pl.ds(pl.multiple_of(base + p * piece, piece), piece)
            xt = xb[rows, :]
            g = jnp.dot(xt, wg, preferred_element_type=f32) + bg_ref[...]
            u = jnp.dot(xt, wu, preferred_element_type=f32) + bu_ref[...]
            g = jnp.minimum(g, SWIGLU_LIMIT)
            u = jnp.clip(u, -SWIGLU_LIMIT, SWIGLU_LIMIT)
            a = (u + 1.0) * (g * jax.nn.sigmoid(SWIGLU_ALPHA * g))
            acc[rows, :] += jnp.dot(a.astype(bf16), wd, preferred_element_type=f32)

    def big_body(q, carry):
        chain(pl.multiple_of(q * (CHAIN_MAX * SUB), CHAIN_MAX * SUB), CHAIN_MAX * SUB)
        return carry
    lax.fori_loop(0, nsub // CHAIN_MAX, big_body, 0)
    units = CHAIN_MAX // 2
    while units >= 1:
        @pl.when((nsub & units) != 0)
        def _(units=units):
            start = (nsub // (2 * units)) * (2 * units)
            chain(pl.multiple_of(start * SUB, units * SUB), units * SUB)
        units //= 2

    @pl.when((j == n_f - 1) & (v + 1 < pl.num_programs(0)))
    def _():
        nxt = jnp.minimum(v + 1, pl.num_programs(0) - 1)
        x_prefetch(vrow_ref[nxt], vnsub_ref[nxt])

    @pl.when((j == n_f - 1) & (nsub > 0))
    def _():
        def body(s, carry):
            slot = s % O_SLOTS

            @pl.when(s >= O_SLOTS)
            def _():
                o_copy(s - O_SLOTS, slot).wait()
            words = _pack_slab_words(acc[pl.ds(pl.multiple_of(s * SUB, SUB), SUB), :] + bd_ref[...])
            for jj in range(slab_rows):
                ostage[pl.ds(slot * stage_rows + jj, SUB, stride=slab_rows), :] = (
                    words[:, jj * LANES:(jj + 1) * LANES])
            o_copy(s, slot).start()
            return carry
        lax.fori_loop(0, nsub, body, 0)

        def drain(s, carry):
            o_copy(s, s % O_SLOTS).wait()
            return carry
        lax.fori_loop(jnp.maximum(nsub - O_SLOTS, 0), nsub, drain, 0)


def _expert(xs, w_gate, b_gate, w_up, b_up, w_down, b_down, vexp, vrow, vnsub, tail, p_rows):
    n_experts, d, f = w_gate.shape
    tf = min(EXPERT_TF, f)
    assert f % tf == 0 and EXPERT_ROWS % SUB == 0
    n_f = f // tf
    slab_rows = d // (2 * LANES)
    n_visits = vexp.shape[0]

    def w_idx(v, j, ve, vr, vn, tl):
        return jnp.where(vn[v] == 0, n_f - 1, j)

    kernel = functools.partial(_expert_kernel, n_f=n_f)
    return pl.pallas_call(
        kernel,
        grid_spec=pltpu.PrefetchScalarGridSpec(
            num_scalar_prefetch=4,
            grid=(n_visits, n_f),
            in_specs=[
                pl.BlockSpec(memory_space=pl.ANY),
                pl.BlockSpec((None, d, tf), lambda v, j, ve, *s: (ve[v], 0, w_idx(v, j, ve, *s))),
                pl.BlockSpec((None, d, tf), lambda v, j, ve, *s: (ve[v], 0, w_idx(v, j, ve, *s))),
                pl.BlockSpec((None, tf, d), lambda v, j, ve, *s: (ve[v], w_idx(v, j, ve, *s), 0)),
                pl.BlockSpec((None, 1, tf), lambda v, j, ve, *s: (ve[v], 0, w_idx(v, j, ve, *s))),
                pl.BlockSpec((None, 1, tf), lambda v, j, ve, *s: (ve[v], 0, w_idx(v, j, ve, *s))),
                pl.BlockSpec((None, 1, d), lambda v, j, ve, *s: (ve[v], 0, 0)),
            ],
            out_specs=pl.BlockSpec(memory_space=pl.ANY),
            scratch_shapes=[pltpu.VMEM((EXPERT_ROWS, d), jnp.bfloat16),
                            pltpu.VMEM((EXPERT_ROWS, d), jnp.float32),
                            pltpu.VMEM((X_SLOTS * SUB * slab_rows, LANES), jnp.uint32),
                            pltpu.VMEM((O_SLOTS * SUB * slab_rows, LANES), jnp.uint32),
                            pltpu.SemaphoreType.DMA((X_SLOTS,)),
                            pltpu.SemaphoreType.DMA((O_SLOTS,)),
                            pltpu.SemaphoreType.DMA(())]),
        out_shape=jax.ShapeDtypeStruct((p_rows * slab_rows, LANES), jnp.uint32),
        compiler_params=_cparams(("arbitrary", "arbitrary")),
        name="expert",
    )(vexp, vrow, vnsub, tail, xs, w_gate, w_up, w_down,
      b_gate.reshape(n_experts, 1, f), b_up.reshape(n_experts, 1, f),
      b_down.reshape(n_experts, 1, d))


def _combine_kernel(dest_ref, dnext_ref, ys_ref, wgt_ref, x1_ref, mod_ref, g_ref, o_ref,
                    buf, ybuf, sem, *, tm):
    i = pl.program_id(0)
    d = x1_ref.shape[1]
    slab_rows = d // (2 * LANES)
    half = TOP_K * tm

    def gather(idx_ref, slot):
        def body(t, carry):
            for u in range(COMBINE_UNROLL):
                tok = t * COMBINE_UNROLL + u
                for k in range(TOP_K):
                    src = _slab(ys_ref, idx_ref[0, 0, tok * TOP_K + k], slab_rows)
                    dst = _slab(buf, slot * half + k * tm + tok, slab_rows)
                    pltpu.make_async_copy(src, dst, sem.at[slot]).start(priority=k % 2)
            return carry
        lax.fori_loop(0, tm // COMBINE_UNROLL, body, 0)

    slot = i % 2

    @pl.when(i == 0)
    def _():
        gather(dest_ref, 0)

    @pl.when(i + 1 < pl.num_programs(0))
    def _():
        gather(dnext_ref, 1 - slot)

    pltpu.make_async_copy(_slab(ys_ref, 0, slab_rows, half), _slab(buf, slot * half, slab_rows, half),
                          sem.at[slot]).wait()

    wgt = wgt_ref[...]
    base = slot * half * slab_rows
    for jj in range(slab_rows):
        ylo = jnp.zeros((tm, LANES), jnp.float32)
        yhi = jnp.zeros((tm, LANES), jnp.float32)
        for k in range(TOP_K):
            lo, hi = _unpack_slab_words(
                buf[pl.ds(base + k * tm * slab_rows + jj, tm, stride=slab_rows), :])
            ylo = ylo + wgt[:, k:k + 1] * lo
            yhi = yhi + wgt[:, k:k + 1] * hi
        ybuf[:, jj * LANES:(jj + 1) * LANES] = ylo
        ybuf[:, d // 2 + jj * LANES:d // 2 + (jj + 1) * LANES] = yhi
    mod = mod_ref[0]
    o_ref[...] = x1_ref[...] + mod[5:6] * _rms(ybuf[...], g_ref[...])


def _combine(ys, dest_flat, wgt, x1, mod3, seq, g_post):
    n, d = x1.shape
    slab_rows = d // (2 * LANES)
    tm = COMBINE_TM
    assert seq % tm == 0
    tiles_per_seq = seq // tm
    n_tiles = n // tm
    dest3 = dest_flat.reshape(n_tiles, 1, tm * TOP_K)
    kernel = functools.partial(_combine_kernel, tm=tm)
    return pl.pallas_call(
        kernel,
        grid=(n_tiles,),
        in_specs=[pl.BlockSpec((1, 1, tm * TOP_K), lambda i: (i, 0, 0), memory_space=pltpu.SMEM),
                  pl.BlockSpec((1, 1, tm * TOP_K), lambda i: (jnp.minimum(i + 1, n_tiles - 1), 0, 0),
                               memory_space=pltpu.SMEM),
                  pl.BlockSpec(memory_space=pl.ANY),
                  pl.BlockSpec((tm, LANES), lambda i: (i, 0)),
                  pl.BlockSpec((tm, d), lambda i: (i, 0)),
                  pl.BlockSpec((1, mod3.shape[1], d), lambda i: (i // tiles_per_seq, 0, 0)),
                  pl.BlockSpec((1, d), lambda i: (0, 0))],
        out_specs=pl.BlockSpec((tm, d), lambda i: (i, 0)),
        out_shape=jax.ShapeDtypeStruct((n, d), jnp.float32),
        scratch_shapes=[pltpu.VMEM((2 * TOP_K * tm * slab_rows, LANES), jnp.uint32),
                        pltpu.VMEM((tm, d), jnp.float32),
                        pltpu.SemaphoreType.DMA((2,))],
        compiler_params=_cparams(("arbitrary",)),
        name="combine",
    )(dest3, dest3, ys, wgt, x1, mod3, g_post)


def _routing_tables(counts, n_assign):
    n_experts = counts.shape[0]
    p_rows = n_assign + n_experts * SUB
    n_visits = n_experts + -(-p_rows // EXPERT_ROWS)

    padded = jnp.maximum((counts + SUB - 1) // SUB, 1) * SUB
    e_ids = jnp.arange(n_experts, dtype=jnp.int32)
    lower = e_ids[None, :] <= e_ids[:, None]
    pad_end = jnp.sum(jnp.where(lower, padded[None, :], 0), axis=1)
    pad_start = pad_end - padded
    total = pad_end[-1]
    zstart = (pad_start + counts).astype(jnp.int32)
    tail_xs = jnp.stack([total, (p_rows + SUB - total) // SUB]).astype(jnp.int32)
    tail_ys = jnp.stack([total, (p_rows - total) // SUB]).astype(jnp.int32)

    n_chunk = (padded + EXPERT_ROWS - 1) // EXPERT_ROWS
    chunk_end = jnp.sum(jnp.where(lower, n_chunk[None, :], 0), axis=1)
    visit = jnp.arange(n_visits, dtype=jnp.int32)
    used = visit < chunk_end[-1]
    vexp = jnp.minimum(jnp.sum((chunk_end[None, :] <= visit[:, None]).astype(jnp.int32), axis=1),
                       n_experts - 1)
    onehot = vexp[:, None] == e_ids[None, :]
    pick = lambda a: jnp.sum(jnp.where(onehot, a[None, :], 0), axis=1)
    chunk = visit - pick(chunk_end - n_chunk)
    vrow = jnp.where(used, pick(pad_start) + chunk * EXPERT_ROWS, 0)
    vnsub = jnp.where(used, jnp.minimum(pick(padded) - chunk * EXPERT_ROWS, EXPERT_ROWS) // SUB, 0)
    return (pad_start.astype(jnp.int32), zstart, tail_xs, tail_ys, vexp.astype(jnp.int32),
            vrow.astype(jnp.int32), vnsub.astype(jnp.int32), p_rows)


def kernel(x, c, w_ada, b_ada, g_pre_mix, g_post_mix, w_in, conv_a_w, conv_b_w, conv_b_b, ln_b_g, ln_b_b, w_out, g_pre_ffn, g_post_ffn, w_router, b_router, w_gate, b_gate, w_up, b_up, w_down, b_down):
    bsz, seq, d = x.shape
    n = bsz * seq
    depth = w_ada.shape[0]
    n_experts = w_router.shape[-1]
    assert n_experts <= LANES
    xf = x.reshape(n, d)
    for l in range(depth):
        mod = _ada(c, w_ada[l], b_ada[l])
        n_mod = mod.shape[1] // d
        mod3 = mod.reshape(bsz, n_mod, d)

        wr = jnp.zeros((d, LANES), jnp.float32).at[:, :n_experts].set(w_router[l])
        wr_hi = wr.astype(jnp.bfloat16)
        wr_lo = (wr - wr_hi.astype(jnp.float32)).astype(jnp.bfloat16)
        br = jnp.full((1, LANES), NEG_BIG, jnp.float32).at[0, :n_experts].set(b_router[l])

        x1, hp, idx, wgt, rank, cnt = _mix(
            xf, mod3, seq, g_pre_mix[l].reshape(1, d), g_post_mix[l].reshape(1, d),
            g_pre_ffn[l].reshape(1, d), w_in[l].astype(jnp.bfloat16), conv_a_w[l], conv_b_w[l],
            conv_b_b[l].reshape(1, -1), ln_b_g[l].reshape(1, -1), ln_b_b[l].reshape(1, -1),
            w_out[l].astype(jnp.bfloat16), jnp.concatenate([wr_hi, wr_lo], axis=1), wr_hi, br)

        (pad_start, zstart, tail_xs, tail_ys, vexp, vrow, vnsub,
         p_rows) = _routing_tables(cnt[0, :n_experts], n * TOP_K)
        start_row = jnp.zeros((1, LANES), jnp.float32).at[0, :n_experts].set(
            pad_start.astype(jnp.float32))
        dest = _plan(idx, rank, start_row)
        dest_flat = dest[:, :TOP_K].reshape(n * TOP_K)

        slab_rows = d // (2 * LANES)
        xs = _dispatch(hp, dest_flat, zstart, tail_xs, p_rows + SUB, slab_rows)
        ys = _expert(xs, w_gate[l], b_gate[l], w_up[l], b_up[l], w_down[l], b_down[l],
                     vexp, vrow, vnsub, tail_ys, p_rows)
        xf = _combine(ys, dest_flat, wgt, x1, mod3, seq, g_post_ffn[l].reshape(1, d))
    return xf.reshape(bsz, seq, d)
```

```python
import functools

import jax
import jax.numpy as jnp
from jax import lax
from jax.experimental import pallas as pl
from jax.experimental.pallas import tpu as pltpu

EPS = 1e-6
TOP_K = 4
CONV_A = 3
CONV_B = 31
SWIGLU_ALPHA = 1.702
SWIGLU_LIMIT = 7.0

LANES = 128
SUBLANES = 8
VMEM_LIMIT_BYTES = 62 * 1024 * 1024

ADA_TN = 1024
MIX_TM = 256
HALO_A = 8
HALO_B = 32
PLAN_TM = 2048
DISPATCH_TM = 1024
DISPATCH_UNROLL = 8
SUB = 128
EXPERT_ROWS = 2176
EXPERT_TF = 512
CHAIN_MAX = 8
CHAIN_PIECE = 8
X_AHEAD = 3
X_SLOTS = X_AHEAD + 1
O_SLOTS = 3
COMBINE_TM = 256
COMBINE_UNROLL = 8
NEG_BIG = -1e30


def _cparams(sem):
    return pltpu.CompilerParams(dimension_semantics=sem, vmem_limit_bytes=VMEM_LIMIT_BYTES)


def _rms(x, g):
    return x * lax.rsqrt(jnp.mean(x * x, axis=-1, keepdims=True) + EPS) * g


def _pack_slab_words(v):
    half = v.shape[1] // 2
    lo = pltpu.bitcast(v[:, :half].astype(jnp.bfloat16).astype(jnp.float32), jnp.uint32)
    hi = pltpu.bitcast(v[:, half:].astype(jnp.bfloat16).astype(jnp.float32), jnp.uint32)
    return (lo >> 16) | (hi & jnp.uint32(0xFFFF0000))


def _unpack_slab_words(w):
    lo = pltpu.bitcast(w << 16, jnp.float32)
    hi = pltpu.bitcast(w & jnp.uint32(0xFFFF0000), jnp.float32)
    return lo, hi


def _ada_kernel(c_ref, w_ref, b_ref, o_ref):
    c = c_ref[...]
    ca = (c * jax.nn.sigmoid(c)).astype(jnp.bfloat16)
    o_ref[...] = jnp.dot(ca, w_ref[...].astype(jnp.bfloat16),
                         preferred_element_type=jnp.float32) + b_ref[...]


def _ada(c, w_ada, b_ada):
    bsz, d = c.shape
    n_out = w_ada.shape[1]
    tn = min(ADA_TN, n_out)
    assert n_out % tn == 0 and bsz <= SUBLANES
    c8 = jnp.zeros((SUBLANES, d), jnp.float32).at[:bsz].set(c)
    out = pl.pallas_call(
        _ada_kernel,
        grid=(n_out // tn,),
        in_specs=[pl.BlockSpec((SUBLANES, d), lambda j: (0, 0)),
                  pl.BlockSpec((d, tn), lambda j: (0, j)),
                  pl.BlockSpec((1, tn), lambda j: (0, j))],
        out_specs=pl.BlockSpec((SUBLANES, tn), lambda j: (0, j)),
        out_shape=jax.ShapeDtypeStruct((SUBLANES, n_out), jnp.float32),
        compiler_params=_cparams(("arbitrary",)),
        name="ada",
    )(c8, w_ada, b_ada.reshape(1, n_out))
    return out[:bsz]


def _mix_kernel(x_ref, xp_ref, mod_ref, modp_ref, gpre_ref, gpost_ref, gffn_ref, win_ref,
                caw_ref, cbw_ref, cbb_ref, lng_ref, lnb_ref, wout_ref, wrc_ref, wrh_ref, br_ref,
                x1_ref, hp_ref, idx_ref, wgt_ref, rank_ref, cnt_ref,
                pa_buf, u_buf, v_buf, ycat, run_cnt, *, tiles_per_seq, wa, wb):
    i = pl.program_id(0)
    tm, d = x_ref.shape
    f32, bf16 = jnp.float32, jnp.bfloat16

    @pl.when(i == 0)
    def _():
        run_cnt[...] = jnp.zeros_like(run_cnt)
        ycat[...] = jnp.zeros_like(ycat)

    @pl.when(i % tiles_per_seq == 0)
    def _():
        pa_buf[:, 0:HALO_A, :] = jnp.zeros((wa // LANES, HALO_A, LANES), f32)
        u_buf[:, 0:HALO_B, :] = jnp.zeros((wb // LANES, HALO_B, LANES), f32)

    y_prev = jnp.dot(ycat[...], wout_ref[...], preferred_element_type=f32)

    x = x_ref[...]
    mod = mod_ref[0]
    h = (_rms(x, gpre_ref[...]) * (1.0 + mod[1:2]) + mod[0:1]).astype(bf16)
    proj_b = jnp.dot(h, win_ref[:, 3 * wa:3 * wa + 2 * wb], preferred_element_type=f32)
    proj_a = jnp.dot(h, win_ref[:, 0:3 * wa], preferred_element_type=f32)

    col = lambda cb, parts, part: slice((cb * parts + part) * LANES, (cb * parts + part + 1) * LANES)

    for cb in range(wa // LANES):
        sl = slice(cb * LANES, (cb + 1) * LANES)
        pa = proj_a[:, col(cb, 3, 0)] * proj_a[:, col(cb, 3, 2)]
        pa_buf[cb, HALO_A:HALO_A + tm, :] = pa
        conv = caw_ref[CONV_A - 1:CONV_A, sl] * pa
        for k in range(CONV_A - 1):
            off = HALO_A - (CONV_A - 1) + k
            conv = conv + caw_ref[k:k + 1, sl] * pa_buf[cb, off:off + tm, :]
        ycat[:, sl] = (proj_a[:, col(cb, 3, 1)] * conv).astype(bf16)
        pa_buf[cb, 0:HALO_A, :] = pa_buf[cb, tm:tm + HALO_A, :]

    for cb in range(wb // LANES):
        sl = slice(cb * LANES, (cb + 1) * LANES)
        u = proj_b[:, col(cb, 2, 0)] * jax.nn.sigmoid(proj_b[:, col(cb, 2, 1)])
        u_buf[cb, HALO_B:HALO_B + tm, :] = u
        acc = cbb_ref[:, sl] + cbw_ref[CONV_B - 1:CONV_B, sl] * u
        for k in range(CONV_B - 1):
            off = HALO_B - (CONV_B - 1) + k
            acc = acc + cbw_ref[k:k + 1, sl] * u_buf[cb, off:off + tm, :]
        v_buf[:, sl] = acc
        u_buf[cb, 0:HALO_B, :] = u_buf[cb, tm:tm + HALO_B, :]
    v = v_buf[...]
    mu = jnp.mean(v, axis=-1, keepdims=True)
    vc = v - mu
    var = jnp.mean(vc * vc, axis=-1, keepdims=True)
    yb = vc * lax.rsqrt(var + EPS) * lng_ref[...] + lnb_ref[...]
    ycat[:, wa:wa + wb] = (yb * jax.nn.sigmoid(yb)).astype(bf16)

    _mix_stage2(i >= 1, y_prev, xp_ref, modp_ref, gpost_ref, gffn_ref, wrc_ref, wrh_ref, br_ref,
                x1_ref, hp_ref, idx_ref, wgt_ref, rank_ref, cnt_ref, run_cnt)


def _mix_stage2(valid, y, x_ref, mod_ref, gpost_ref, gffn_ref, wrc_ref, wrh_ref, br_ref,
                x1_ref, hp_ref, idx_ref, wgt_ref, rank_ref, cnt_ref, run_cnt):
    tm, d = x_ref.shape
    f32, bf16 = jnp.float32, jnp.bfloat16
    x = x_ref[...]
    mod = mod_ref[0]
    x1 = x + mod[2:3] * _rms(y, gpost_ref[...])
    x1_ref[...] = x1

    h2 = _rms(x1, gffn_ref[...]) * (1.0 + mod[4:5]) + mod[3:4]
    words = _pack_slab_words(h2)
    slab_rows = d // (2 * LANES)
    for j in range(slab_rows):
        hp_ref[pl.ds(j, tm, stride=slab_rows), :] = words[:, j * LANES:(j + 1) * LANES]

    h2_hi = h2.astype(bf16)
    h2_lo = (h2 - h2_hi.astype(f32)).astype(bf16)
    both = jnp.dot(h2_hi, wrc_ref[...], preferred_element_type=f32)
    logits = (both[:, :LANES] + both[:, LANES:]
              + jnp.dot(h2_lo, wrh_ref[...], preferred_element_type=f32)
              + br_ref[...])

    lane = lax.broadcasted_iota(jnp.int32, (tm, LANES), 1)
    lane_f = lane.astype(f32)
    vals, idxs = [], []
    cur = logits
    for _ in range(TOP_K):
        m = jnp.max(cur, axis=-1, keepdims=True)
        ix = jnp.min(jnp.where(cur == m, lane_f, float(LANES)), axis=-1,
                     keepdims=True).astype(jnp.int32)
        vals.append(m)
        idxs.append(ix)
        cur = jnp.where(lane == ix, -jnp.inf, cur)
    exps = [jnp.exp(vk - vals[0]) for vk in vals]
    denom = exps[0]
    for ek in exps[1:]:
        denom = denom + ek
    inv = 1.0 / denom

    row = lax.broadcasted_iota(jnp.int32, (tm, tm), 0)
    col = lax.broadcasted_iota(jnp.int32, (tm, tm), 1)
    tri = jnp.where(col < row, 1.0, 0.0).astype(bf16)
    run = run_cnt[...]
    idx_out = jnp.zeros((tm, LANES), jnp.int32)
    wgt_out = jnp.zeros((tm, LANES), f32)
    rank_out = jnp.zeros((tm, LANES), f32)
    onehots = [jnp.where(lane == idxs[k], 1.0, 0.0) for k in range(TOP_K)]
    before_all = jnp.dot(tri, jnp.concatenate(onehots, axis=1).astype(bf16),
                         preferred_element_type=f32)
    for k in range(TOP_K):
        oh = onehots[k]
        before = before_all[:, k * LANES:(k + 1) * LANES]
        rank_k = jnp.sum(oh * (before + run), axis=-1, keepdims=True)
        run = run + jnp.sum(oh, axis=0, keepdims=True)
        idx_out = jnp.where(lane == k, idxs[k], idx_out)
        wgt_out = jnp.where(lane == k, exps[k] * inv, wgt_out)
        rank_out = jnp.where(lane == k, rank_k, rank_out)
    run = jnp.where(valid, run, run_cnt[...])
    run_cnt[...] = run
    idx_ref[...] = idx_out
    wgt_ref[...] = wgt_out
    rank_ref[...] = rank_out.astype(jnp.int32)
    cnt_ref[...] = jnp.broadcast_to(run, cnt_ref.shape).astype(jnp.int32)


def _mix(x2, mod3, seq, g_pre, g_post, g_ffn, w_in, conv_a_w, conv_b_w, conv_b_b, ln_g, ln_b,
         w_out, wr_cat, wr_hi, br):
    n, d = x2.shape
    wa = conv_a_w.shape[1]
    wb = conv_b_w.shape[1]
    tm = MIX_TM
    assert seq % tm == 0 and tm >= HALO_B and wa % LANES == 0 and wb % LANES == 0
    assert wa + wb == d and d % (2 * LANES) == 0
    tiles_per_seq = seq // tm
    n_tiles = n // tm
    slab_rows = d // (2 * LANES)
    cur = lambda i: jnp.minimum(i, n_tiles - 1)
    prev = lambda i: jnp.maximum(i - 1, 0)
    const = lambda shape: pl.BlockSpec(shape, lambda i: (0,) * len(shape))
    resident = lambda shape: pl.BlockSpec(shape, lambda i: (0,) * len(shape),
                                          pipeline_mode=pl.Buffered(1))
    out_block = lambda w: pl.BlockSpec((tm, w), lambda i: (prev(i), 0))
    mod_block = lambda tile: pl.BlockSpec((1, mod3.shape[1], d),
                                          lambda i: (tile(i) // tiles_per_seq, 0, 0))
    kernel = functools.partial(_mix_kernel, tiles_per_seq=tiles_per_seq, wa=wa, wb=wb)
    return pl.pallas_call(
        kernel,
        grid=(n_tiles + 1,),
        in_specs=[pl.BlockSpec((tm, d), lambda i: (cur(i), 0)),
                  pl.BlockSpec((tm, d), lambda i: (prev(i), 0)),
                  mod_block(cur), mod_block(prev),
                  const((1, d)), const((1, d)), const((1, d)),
                  resident(w_in.shape),
                  const(conv_a_w.shape), const(conv_b_w.shape),
                  const((1, wb)), const((1, wb)), const((1, wb)),
                  resident(w_out.shape),
                  const(wr_cat.shape), const(wr_hi.shape), const((1, LANES))],
        out_specs=[out_block(d),
                   pl.BlockSpec((tm * slab_rows, LANES), lambda i: (prev(i), 0)),
                   out_block(LANES), out_block(LANES), out_block(LANES),
                   const((SUBLANES, LANES))],
        out_shape=[jax.ShapeDtypeStruct((n, d), jnp.float32),
                   jax.ShapeDtypeStruct((n * slab_rows, LANES), jnp.uint32),
                   jax.ShapeDtypeStruct((n, LANES), jnp.int32),
                   jax.ShapeDtypeStruct((n, LANES), jnp.float32),
                   jax.ShapeDtypeStruct((n, LANES), jnp.int32),
                   jax.ShapeDtypeStruct((SUBLANES, LANES), jnp.int32)],
        scratch_shapes=[pltpu.VMEM((wa // LANES, tm + HALO_A, LANES), jnp.float32),
                        pltpu.VMEM((wb // LANES, tm + HALO_B, LANES), jnp.float32),
                        pltpu.VMEM((tm, wb), jnp.float32),
                        pltpu.VMEM((tm, d), jnp.bfloat16),
                        pltpu.VMEM((1, LANES), jnp.float32)],
        compiler_params=_cparams(("arbitrary",)),
        name="mix",
    )(x2, x2, mod3, mod3, g_pre, g_post, g_ffn, w_in, conv_a_w, conv_b_w, conv_b_b, ln_g, ln_b,
      w_out, wr_cat, wr_hi, br)


def _plan_kernel(idx_ref, rank_ref, start_ref, dest_ref):
    tm = idx_ref.shape[0]
    start = jnp.broadcast_to(start_ref[...], (tm, LANES))
    dest_ref[...] = jnp.take_along_axis(start, idx_ref[...], axis=1) + rank_ref[...]


def _plan(idx, rank, start_row):
    n = idx.shape[0]
    tm = min(PLAN_TM, n)
    assert n % tm == 0
    blk = pl.BlockSpec((tm, LANES), lambda i: (i, 0))
    return pl.pallas_call(
        _plan_kernel,
        grid=(n // tm,),
        in_specs=[blk, blk, pl.BlockSpec((1, LANES), lambda i: (0, 0))],
        out_specs=blk,
        out_shape=jax.ShapeDtypeStruct((n, LANES), jnp.int32),
        compiler_params=_cparams(("arbitrary",)),
        name="plan",
    )(idx, rank, start_row)


def _slab(ref, row, slab_rows, count=1):
    return ref.at[pl.ds(pl.multiple_of(row * slab_rows, slab_rows), count * slab_rows)]


def _dispatch_kernel(zstart_ref, tail_ref, dest_ref, hp_ref, xs_ref, zbuf, sem, zsem,
                     *, tm, n_experts, slab_rows):
    i = pl.program_id(0)

    def zero_copy(start):
        return pltpu.make_async_copy(zbuf, _slab(xs_ref, start, slab_rows, SUB), zsem)

    @pl.when(i == 0)
    def _():
        zbuf[...] = jnp.zeros_like(zbuf)
        for parity in range(2):
            for e in range(parity, n_experts, 2):
                zero_copy(zstart_ref[e]).start()
            for e in range(parity, n_experts, 2):
                zero_copy(zstart_ref[e]).wait()

        def tail_start(t, carry):
            zero_copy(tail_ref[0] + t * SUB).start()
            return carry

        def tail_wait(t, carry):
            zero_copy(tail_ref[0] + t * SUB).wait()
            return carry
        lax.fori_loop(0, tail_ref[1], tail_start, 0)
        lax.fori_loop(0, tail_ref[1], tail_wait, 0)

    def body(t, carry):
        for u in range(DISPATCH_UNROLL):
            tok = t * DISPATCH_UNROLL + u
            src = _slab(hp_ref, tok, slab_rows)
            for k in range(TOP_K):
                dst = _slab(xs_ref, dest_ref[0, 0, tok * TOP_K + k], slab_rows)
                pltpu.make_async_copy(src, dst, sem).start(priority=k % 2)
        return carry
    lax.fori_loop(0, tm // DISPATCH_UNROLL, body, 0)
    for k in range(TOP_K):
        pltpu.make_async_copy(hp_ref, _slab(xs_ref, 0, slab_rows, tm), sem).wait()


def _dispatch(hp, dest_flat, zstart, tail, p_rows, slab_rows):
    n = hp.shape[0] // slab_rows
    tm = min(DISPATCH_TM, n)
    assert n % tm == 0 and tm % DISPATCH_UNROLL == 0
    n_experts = zstart.shape[0]
    dest3 = dest_flat.reshape(n // tm, 1, tm * TOP_K)
    kernel = functools.partial(_dispatch_kernel, tm=tm, n_experts=n_experts, slab_rows=slab_rows)
    return pl.pallas_call(
        kernel,
        grid_spec=pltpu.PrefetchScalarGridSpec(
            num_scalar_prefetch=2,
            grid=(n // tm,),
            in_specs=[pl.BlockSpec((1, 1, tm * TOP_K), lambda i, *_: (i, 0, 0),
                                   memory_space=pltpu.SMEM),
                      pl.BlockSpec((tm * slab_rows, LANES), lambda i, *_: (i, 0))],
            out_specs=pl.BlockSpec(memory_space=pl.ANY),
            scratch_shapes=[pltpu.VMEM((SUB * slab_rows, LANES), jnp.uint32),
                            pltpu.SemaphoreType.DMA(()),
                            pltpu.SemaphoreType.DMA(())]),
        out_shape=jax.ShapeDtypeStruct((p_rows * slab_rows, LANES), jnp.uint32),
        compiler_params=_cparams(("arbitrary",)),
        name="dispatch",
    )(zstart, tail, dest3, hp)


def _expert_kernel(vexp_ref, vrow_ref, vnsub_ref, tail_ref,
                   xs_ref, wg_ref, wu_ref, wd_ref, bg_ref, bu_ref, bd_ref, ys_ref,
                   xb, acc, xstage, ostage, xsem, osem, zsem, *, n_f):
    del vexp_ref
    v = pl.program_id(0)
    j = pl.program_id(1)
    d = xb.shape[1]
    slab_rows = d // (2 * LANES)
    stage_rows = SUB * slab_rows
    f32, bf16 = jnp.float32, jnp.bfloat16
    nsub = vnsub_ref[v]
    row0 = vrow_ref[v]

    def stage_slot(ref, slot):
        return ref.at[pl.ds(pl.multiple_of(slot * stage_rows, stage_rows), stage_rows)]

    def x_copy(first_row, s):
        slot = s % X_SLOTS
        return pltpu.make_async_copy(_slab(xs_ref, first_row + s * SUB, slab_rows, SUB),
                                     stage_slot(xstage, slot), xsem.at[slot])

    def x_prefetch(first_row, count):
        for s in range(X_AHEAD):
            @pl.when(s < count)
            def _():
                x_copy(first_row, s).start()

    def o_copy(s, slot):
        return pltpu.make_async_copy(stage_slot(ostage, slot),
                                     _slab(ys_ref, row0 + s * SUB, slab_rows, SUB), osem.at[slot])

    @pl.when((v == 0) & (j == 0))
    def _():
        ostage[0:stage_rows, :] = jnp.zeros((stage_rows, LANES), jnp.uint32)

        def z_copy(t):
            return pltpu.make_async_copy(stage_slot(ostage, 0),
                                         _slab(ys_ref, tail_ref[0] + t * SUB, slab_rows, SUB), zsem)

        def z_start(t, carry):
            z_copy(t).start()
            return carry

        def z_wait(t, carry):
            z_copy(t).wait()
            return carry
        lax.fori_loop(0, tail_ref[1], z_start, 0)
        lax.fori_loop(0, tail_ref[1], z_wait, 0)
        x_prefetch(row0, nsub)

    @pl.when(j == 0)
    def _():
        def body(s, carry):
            slot = s % X_SLOTS

            @pl.when(s + X_AHEAD < nsub)
            def _():
                x_copy(row0, s + X_AHEAD).start()
            x_copy(row0, s).wait()
            base = pl.multiple_of(s * SUB, SUB)
            for jj in range(slab_rows):
                lo, hi = _unpack_slab_words(
                    xstage[pl.ds(slot * stage_rows + jj, SUB, stride=slab_rows), :])
                xb[pl.ds(base, SUB), jj * LANES:(jj + 1) * LANES] = lo.astype(bf16)
                xb[pl.ds(base, SUB), d // 2 + jj * LANES:d // 2 + (jj + 1) * LANES] = hi.astype(bf16)
            acc[pl.ds(base, SUB), :] = jnp.zeros((SUB, d), f32)
            return carry
        lax.fori_loop(0, nsub, body, 0)

    def chain(base, m):
        wg = wg_ref[...].astype(bf16)
        wu = wu_ref[...].astype(bf16)
        wd = wd_ref[...].astype(bf16)
        piece = min(m, CHAIN_PIECE * SUB)
        for p in range(m // piece):
            rows = pl.ds(pl.multiple_of(base + p * piece, piece), piece)
            xt = xb[rows, :]
            g = jnp.dot(xt, wg, preferred_element_type=f32) + bg_ref[...]
            u = jnp.dot(xt, wu, preferred_element_type=f32) + bu_ref[...]
            g = jnp.minimum(g, SWIGLU_LIMIT)
            u = jnp.clip(u, -SWIGLU_LIMIT, SWIGLU_LIMIT)
            a = (u + 1.0) * (g * jax.nn.sigmoid(SWIGLU_ALPHA * g))
            acc[rows, :] += jnp.dot(a.astype(bf16), wd, preferred_element_type=f32)

    def big_body(q, carry):
        chain(pl.multiple_of(q * (CHAIN_MAX * SUB), CHAIN_MAX * SUB), CHAIN_MAX * SUB)
        return carry
    lax.fori_loop(0, nsub // CHAIN_MAX, big_body, 0)
    units = CHAIN_MAX // 2
    while units >= 1:
        @pl.when((nsub & units) != 0)
        def _(units=units):
            start = (nsub // (2 * units)) * (2 * units)
            chain(pl.multiple_of(start * SUB, units * SUB), units * SUB)
        units //= 2

    @pl.when((j == n_f - 1) & (v + 1 < pl.num_programs(0)))
    def _():
        nxt = jnp.minimum(v + 1, pl.num_programs(0) - 1)
        x_prefetch(vrow_ref[nxt], vnsub_ref[nxt])

    @pl.when((j == n_f - 1) & (nsub > 0))
    def _():
        def body(s, carry):
            slot = s % O_SLOTS

            @pl.when(s >= O_SLOTS)
            def _():
                o_copy(s - O_SLOTS, slot).wait()
            words = _pack_slab_words(acc[pl.ds(pl.multiple_of(s * SUB, SUB), SUB), :] + bd_ref[...])
            for jj in range(slab_rows):
                ostage[pl.ds(slot * stage_rows + jj, SUB, stride=slab_rows), :] = (
                    words[:, jj * LANES:(jj + 1) * LANES])
            o_copy(s, slot).start()
            return carry
        lax.fori_loop(0, nsub, body, 0)

        def drain(s, carry):
            o_copy(s, s % O_SLOTS).wait()
            return carry
        lax.fori_loop(jnp.maximum(nsub - O_SLOTS, 0), nsub, drain, 0)


def _expert(xs, w_gate, b_gate, w_up, b_up, w_down, b_down, vexp, vrow, vnsub, tail, p_rows):
    n_experts, d, f = w_gate.shape
    tf = min(EXPERT_TF, f)
    assert f % tf == 0 and EXPERT_ROWS % SUB == 0
    n_f = f // tf
    slab_rows = d // (2 * LANES)
    n_visits = vexp.shape[0]

    def w_idx(v, j, ve, vr, vn, tl):
        return jnp.where(vn[v] == 0, n_f - 1, j)

    kernel = functools.partial(_expert_kernel, n_f=n_f)
    return pl.pallas_call(
        kernel,
        grid_spec=pltpu.PrefetchScalarGridSpec(
            num_scalar_prefetch=4,
            grid=(n_visits, n_f),
            in_specs=[
                pl.BlockSpec(memory_space=pl.ANY),
                pl.BlockSpec((None, d, tf), lambda v, j, ve, *s: (ve[v], 0, w_idx(v, j, ve, *s))),
                pl.BlockSpec((None, d, tf), lambda v, j, ve, *s: (ve[v], 0, w_idx(v, j, ve, *s))),
                pl.BlockSpec((None, tf, d), lambda v, j, ve, *s: (ve[v], w_idx(v, j, ve, *s), 0)),
                pl.BlockSpec((None, 1, tf), lambda v, j, ve, *s: (ve[v], 0, w_idx(v, j, ve, *s))),
                pl.BlockSpec((None, 1, tf), lambda v, j, ve, *s: (ve[v], 0, w_idx(v, j, ve, *s))),
                pl.BlockSpec((None, 1, d), lambda v, j, ve, *s: (ve[v], 0, 0)),
            ],
            out_specs=pl.BlockSpec(memory_space=pl.ANY),
            scratch_shapes=[pltpu.VMEM((EXPERT_ROWS, d), jnp.bfloat16),
                            pltpu.VMEM((EXPERT_ROWS, d), jnp.float32),
                            pltpu.VMEM((X_SLOTS * SUB * slab_rows, LANES), jnp.uint32),
                            pltpu.VMEM((O_SLOTS * SUB * slab_rows, LANES), jnp.uint32),
                            pltpu.SemaphoreType.DMA((X_SLOTS,)),
                            pltpu.SemaphoreType.DMA((O_SLOTS,)),
                            pltpu.SemaphoreType.DMA(())]),
        out_shape=jax.ShapeDtypeStruct((p_rows * slab_rows, LANES), jnp.uint32),
        compiler_params=_cparams(("arbitrary", "arbitrary")),
        name="expert",
    )(vexp, vrow, vnsub, tail, xs, w_gate, w_up, w_down,
      b_gate.reshape(n_experts, 1, f), b_up.reshape(n_experts, 1, f),
      b_down.reshape(n_experts, 1, d))


def _combine_kernel(dest_ref, dnext_ref, ys_ref, wgt_ref, x1_ref, mod_ref, g_ref, o_ref,
                    buf, ybuf, sem, *, tm):
    i = pl.program_id(0)
    d = x1_ref.shape[1]
    slab_rows = d // (2 * LANES)
    half = TOP_K * tm

    def gather(idx_ref, slot):
        def body(t, carry):
            for u in range(COMBINE_UNROLL):
                tok = t * COMBINE_UNROLL + u
                for k in range(TOP_K):
                    src = _slab(ys_ref, idx_ref[0, 0, tok * TOP_K + k], slab_rows)
                    dst = _slab(buf, slot * half + k * tm + tok, slab_rows)
                    pltpu.make_async_copy(src, dst, sem.at[slot]).start(priority=k % 2)
            return carry
        lax.fori_loop(0, tm // COMBINE_UNROLL, body, 0)

    slot = i % 2

    @pl.when(i == 0)
    def _():
        gather(dest_ref, 0)

    @pl.when(i + 1 < pl.num_programs(0))
    def _():
        gather(dnext_ref, 1 - slot)

    pltpu.make_async_copy(_slab(ys_ref, 0, slab_rows, half), _slab(buf, slot * half, slab_rows, half),
                          sem.at[slot]).wait()

    wgt = wgt_ref[...]
    base = slot * half * slab_rows
    for jj in range(slab_rows):
        ylo = jnp.zeros((tm, LANES), jnp.float32)
        yhi = jnp.zeros((tm, LANES), jnp.float32)
        for k in range(TOP_K):
            lo, hi = _unpack_slab_words(
                buf[pl.ds(base + k * tm * slab_rows + jj, tm, stride=slab_rows), :])
            ylo = ylo + wgt[:, k:k + 1] * lo
            yhi = yhi + wgt[:, k:k + 1] * hi
        ybuf[:, jj * LANES:(jj + 1) * LANES] = ylo
        ybuf[:, d // 2 + jj * LANES:d // 2 + (jj + 1) * LANES] = yhi
    mod = mod_ref[0]
    o_ref[...] = x1_ref[...] + mod[5:6] * _rms(ybuf[...], g_ref[...])


def _combine(ys, dest_flat, wgt, x1, mod3, seq, g_post):
    n, d = x1.shape
    slab_rows = d // (2 * LANES)
    tm = COMBINE_TM
    assert seq % tm == 0
    tiles_per_seq = seq // tm
    n_tiles = n // tm
    dest3 = dest_flat.reshape(n_tiles, 1, tm * TOP_K)
    kernel = functools.partial(_combine_kernel, tm=tm)
    return pl.pallas_call(
        kernel,
        grid=(n_tiles,),
        in_specs=[pl.BlockSpec((1, 1, tm * TOP_K), lambda i: (i, 0, 0), memory_space=pltpu.SMEM),
                  pl.BlockSpec((1, 1, tm * TOP_K), lambda i: (jnp.minimum(i + 1, n_tiles - 1), 0, 0),
                               memory_space=pltpu.SMEM),
                  pl.BlockSpec(memory_space=pl.ANY),
                  pl.BlockSpec((tm, LANES), lambda i: (i, 0)),
                  pl.BlockSpec((tm, d), lambda i: (i, 0)),
                  pl.BlockSpec((1, mod3.shape[1], d), lambda i: (i // tiles_per_seq, 0, 0)),
                  pl.BlockSpec((1, d), lambda i: (0, 0))],
        out_specs=pl.BlockSpec((tm, d), lambda i: (i, 0)),
        out_shape=jax.ShapeDtypeStruct((n, d), jnp.float32),
        scratch_shapes=[pltpu.VMEM((2 * TOP_K * tm * slab_rows, LANES), jnp.uint32),
                        pltpu.VMEM((tm, d), jnp.float32),
                        pltpu.SemaphoreType.DMA((2,))],
        compiler_params=_cparams(("arbitrary",)),
        name="combine",
    )(dest3, dest3, ys, wgt, x1, mod3, g_post)


def _routing_tables(counts, n_assign):
    n_experts = counts.shape[0]
    p_rows = n_assign + n_experts * SUB
    n_visits = n_experts + -(-p_rows // EXPERT_ROWS)

    padded = jnp.maximum((counts + SUB - 1) // SUB, 1) * SUB
    e_ids = jnp.arange(n_experts, dtype=jnp.int32)
    lower = e_ids[None, :] <= e_ids[:, None]
    pad_end = jnp.sum(jnp.where(lower, padded[None, :], 0), axis=1)
    pad_start = pad_end - padded
    total = pad_end[-1]
    zstart = (pad_start + counts).astype(jnp.int32)
    tail_xs = jnp.stack([total, (p_rows + SUB - total) // SUB]).astype(jnp.int32)
    tail_ys = jnp.stack([total, (p_rows - total) // SUB]).astype(jnp.int32)

    n_chunk = (padded + EXPERT_ROWS - 1) // EXPERT_ROWS
    chunk_end = jnp.sum(jnp.where(lower, n_chunk[None, :], 0), axis=1)
    visit = jnp.arange(n_visits, dtype=jnp.int32)
    used = visit < chunk_end[-1]
    vexp = jnp.minimum(jnp.sum((chunk_end[None, :] <= visit[:, None]).astype(jnp.int32), axis=1),
                       n_experts - 1)
    onehot = vexp[:, None] == e_ids[None, :]
    pick = lambda a: jnp.sum(jnp.where(onehot, a[None, :], 0), axis=1)
    chunk = visit - pick(chunk_end - n_chunk)
    vrow = jnp.where(used, pick(pad_start) + chunk * EXPERT_ROWS, 0)
    vnsub = jnp.where(used, jnp.minimum(pick(padded) - chunk * EXPERT_ROWS, EXPERT_ROWS) // SUB, 0)
    return (pad_start.astype(jnp.int32), zstart, tail_xs, tail_ys, vexp.astype(jnp.int32),
            vrow.astype(jnp.int32), vnsub.astype(jnp.int32), p_rows)


def _group_columns(w_in, wa):
    d, width = w_in.shape
    wb = (width - 3 * wa) // 2
    part_a = w_in[:, :3 * wa].reshape(d, 3, wa // LANES, LANES).transpose(0, 2, 1, 3)
    part_b = w_in[:, 3 * wa:].reshape(d, 2, wb // LANES, LANES).transpose(0, 2, 1, 3)
    return jnp.concatenate([part_a.reshape(d, 3 * wa), part_b.reshape(d, 2 * wb)],
                           axis=1).astype(jnp.bfloat16)


def kernel(x, c, w_ada, b_ada, g_pre_mix, g_post_mix, w_in, conv_a_w, conv_b_w, conv_b_b, ln_b_g, ln_b_b, w_out, g_pre_ffn, g_post_ffn, w_router, b_router, w_gate, b_gate, w_up, b_up, w_down, b_down):
    bsz, seq, d = x.shape
    n = bsz * seq
    depth = w_ada.shape[0]
    n_experts = w_router.shape[-1]
    assert n_experts <= LANES
    xf = x.reshape(n, d)
    for l in range(depth):
        mod = _ada(c, w_ada[l], b_ada[l])
        n_mod = mod.shape[1] // d
        mod3 = mod.reshape(bsz, n_mod, d)

        wr = jnp.zeros((d, LANES), jnp.float32).at[:, :n_experts].set(w_router[l])
        wr_hi = wr.astype(jnp.bfloat16)
        wr_lo = (wr - wr_hi.astype(jnp.float32)).astype(jnp.bfloat16)
        br = jnp.full((1, LANES), NEG_BIG, jnp.float32).at[0, :n_experts].set(b_router[l])

        x1, hp, idx, wgt, rank, cnt = _mix(
            xf, mod3, seq, g_pre_mix[l].reshape(1, d), g_post_mix[l].reshape(1, d),
            g_pre_ffn[l].reshape(1, d), _group_columns(w_in[l], conv_a_w.shape[-1]), conv_a_w[l],
            conv_b_w[l],
            conv_b_b[l].reshape(1, -1), ln_b_g[l].reshape(1, -1), ln_b_b[l].reshape(1, -1),
            w_out[l].astype(jnp.bfloat16), jnp.concatenate([wr_hi, wr_lo], axis=1), wr_hi, br)

        (pad_start, zstart, tail_xs, tail_ys, vexp, vrow, vnsub,
         p_rows) = _routing_tables(cnt[0, :n_experts], n * TOP_K)
        start_row = jnp.zeros((1, LANES), jnp.int32).at[0, :n_experts].set(pad_start)
        dest = _plan(idx, rank, start_row)
        dest_flat = dest[:, :TOP_K].reshape(n * TOP_K)

        slab_rows = d // (2 * LANES)
        xs = _dispatch(hp, dest_flat, zstart, tail_xs, p_rows + SUB, slab_rows)
        ys = _expert(xs, w_gate[l], b_gate[l], w_up[l], b_up[l], w_down[l], b_down[l],
                     vexp, vrow, vnsub, tail_ys, p_rows)
        xf = _combine(ys, dest_flat, wgt, x1, mod3, seq, g_post_ffn[l].reshape(1, d))
    return xf.reshape(bsz, seq, d)
```

```python
import functools

import jax
import jax.numpy as jnp
from jax import lax
from jax.experimental import pallas as pl
from jax.experimental.pallas import tpu as pltpu

EPS = 1e-6
TOP_K = 4
CONV_A = 3
CONV_B = 31
SWIGLU_ALPHA = 1.702
SWIGLU_LIMIT = 7.0

LANES = 128
SUBLANES = 8
VMEM_LIMIT_BYTES = 62 * 1024 * 1024

ADA_TN = 1024
MIX_TM = 256
HALO_A = 8
HALO_B = 32
PLAN_TM = 2048
DISPATCH_TM = 1024
DISPATCH_UNROLL = 8
SUB = 128
EXPERT_ROWS = 2176
EXPERT_TF = 512
CHAIN_MAX = 8
CHAIN_PIECE = 8
X_AHEAD = 3
X_SLOTS = X_AHEAD + 1
O_SLOTS = 3
COMBINE_TM = 256
COMBINE_UNROLL = 8
NEG_BIG = -1e30


def _cparams(sem):
    return pltpu.CompilerParams(dimension_semantics=sem, vmem_limit_bytes=VMEM_LIMIT_BYTES)


def _rms(x, g):
    return x * lax.rsqrt(jnp.mean(x * x, axis=-1, keepdims=True) + EPS) * g


def _pack_slab_words(v):
    half = v.shape[1] // 2
    lo = pltpu.bitcast(v[:, :half].astype(jnp.bfloat16).astype(jnp.float32), jnp.uint32)
    hi = pltpu.bitcast(v[:, half:].astype(jnp.bfloat16).astype(jnp.float32), jnp.uint32)
    return (lo >> 16) | (hi & jnp.uint32(0xFFFF0000))


def _unpack_slab_words(w):
    lo = pltpu.bitcast(w << 16, jnp.float32)
    hi = pltpu.bitcast(w & jnp.uint32(0xFFFF0000), jnp.float32)
    return lo, hi


def _ada_kernel(c_ref, w_ref, b_ref, o_ref):
    c = c_ref[...]
    ca = (c * jax.nn.sigmoid(c)).astype(jnp.bfloat16)
    o_ref[...] = jnp.dot(ca, w_ref[...].astype(jnp.bfloat16),
                         preferred_element_type=jnp.float32) + b_ref[...]


def _ada(c, w_ada, b_ada):
    bsz, d = c.shape
    n_out = w_ada.shape[1]
    tn = min(ADA_TN, n_out)
    assert n_out % tn == 0 and bsz <= SUBLANES
    c8 = jnp.zeros((SUBLANES, d), jnp.float32).at[:bsz].set(c)
    out = pl.pallas_call(
        _ada_kernel,
        grid=(n_out // tn,),
        in_specs=[pl.BlockSpec((SUBLANES, d), lambda j: (0, 0)),
                  pl.BlockSpec((d, tn), lambda j: (0, j)),
                  pl.BlockSpec((1, tn), lambda j: (0, j))],
        out_specs=pl.BlockSpec((SUBLANES, tn), lambda j: (0, j)),
        out_shape=jax.ShapeDtypeStruct((SUBLANES, n_out), jnp.float32),
        compiler_params=_cparams(("arbitrary",)),
        name="ada",
    )(c8, w_ada, b_ada.reshape(1, n_out))
    return out[:bsz]


def _mix_kernel(x_ref, xp_ref, mod_ref, modp_ref, gpre_ref, gpost_ref, gffn_ref, win_ref,
                caw_ref, cbw_ref, cbb_ref, lng_ref, lnb_ref, wout_ref, wrc_ref, wrh_ref, br_ref,
                x1_ref, hp_ref, idx_ref, wgt_ref, rank_ref, cnt_ref,
                pa_buf, u_buf, v_buf, ycat, run_cnt, *, tiles_per_seq, wa, wb):
    i = pl.program_id(0)
    tm, d = x_ref.shape
    f32, bf16 = jnp.float32, jnp.bfloat16

    @pl.when(i == 0)
    def _():
        run_cnt[...] = jnp.zeros_like(run_cnt)
        ycat[...] = jnp.zeros_like(ycat)

    @pl.when(i % tiles_per_seq == 0)
    def _():
        pa_buf[:, 0:HALO_A, :] = jnp.zeros((wa // LANES, HALO_A, LANES), f32)
        u_buf[:, 0:HALO_B, :] = jnp.zeros((wb // LANES, HALO_B, LANES), f32)

    y_prev = jnp.dot(ycat[...], wout_ref[...], preferred_element_type=f32)

    x = x_ref[...]
    mod = mod_ref[0]
    h = (_rms(x, gpre_ref[...]) * (1.0 + mod[1:2]) + mod[0:1]).astype(bf16)
    proj_b = jnp.dot(h, win_ref[:, 3 * wa:3 * wa + 2 * wb], preferred_element_type=f32)
    proj_a = jnp.dot(h, win_ref[:, 0:3 * wa], preferred_element_type=f32)

    for cb in range(wa // LANES):
        sl = slice(cb * LANES, (cb + 1) * LANES)
        pa = proj_a[:, sl] * proj_a[:, 2 * wa + cb * LANES:2 * wa + (cb + 1) * LANES]
        pa_buf[cb, HALO_A:HALO_A + tm, :] = pa
        conv = caw_ref[CONV_A - 1:CONV_A, sl] * pa
        for k in range(CONV_A - 1):
            off = HALO_A - (CONV_A - 1) + k
            conv = conv + caw_ref[k:k + 1, sl] * pa_buf[cb, off:off + tm, :]
        ycat[:, sl] = (proj_a[:, wa + cb * LANES:wa + (cb + 1) * LANES] * conv).astype(bf16)
        pa_buf[cb, 0:HALO_A, :] = pa_buf[cb, tm:tm + HALO_A, :]

    for cb in range(wb // LANES):
        sl = slice(cb * LANES, (cb + 1) * LANES)
        u = proj_b[:, sl] * jax.nn.sigmoid(proj_b[:, wb + cb * LANES:wb + (cb + 1) * LANES])
        u_buf[cb, HALO_B:HALO_B + tm, :] = u
        acc = cbb_ref[:, sl] + cbw_ref[CONV_B - 1:CONV_B, sl] * u
        for k in range(CONV_B - 1):
            off = HALO_B - (CONV_B - 1) + k
            acc = acc + cbw_ref[k:k + 1, sl] * u_buf[cb, off:off + tm, :]
        v_buf[:, sl] = acc
        u_buf[cb, 0:HALO_B, :] = u_buf[cb, tm:tm + HALO_B, :]
    v = v_buf[...]
    mu = jnp.mean(v, axis=-1, keepdims=True)
    vc = v - mu
    var = jnp.mean(vc * vc, axis=-1, keepdims=True)
    yb = vc * lax.rsqrt(var + EPS) * lng_ref[...] + lnb_ref[...]
    ycat[:, wa:wa + wb] = (yb * jax.nn.sigmoid(yb)).astype(bf16)

    _mix_stage2(i >= 1, y_prev, xp_ref, modp_ref, gpost_ref, gffn_ref, wrc_ref, wrh_ref, br_ref,
                x1_ref, hp_ref, idx_ref, wgt_ref, rank_ref, cnt_ref, run_cnt)


def _mix_stage2(valid, y, x_ref, mod_ref, gpost_ref, gffn_ref, wrc_ref, wrh_ref, br_ref,
                x1_ref, hp_ref, idx_ref, wgt_ref, rank_ref, cnt_ref, run_cnt):
    tm, d = x_ref.shape
    f32, bf16 = jnp.float32, jnp.bfloat16
    x = x_ref[...]
    mod = mod_ref[0]
    x1 = x + mod[2:3] * _rms(y, gpost_ref[...])
    x1_ref[...] = x1

    h2 = _rms(x1, gffn_ref[...]) * (1.0 + mod[4:5]) + mod[3:4]
    words = _pack_slab_words(h2)
    slab_rows = d // (2 * LANES)
    for j in range(slab_rows):
        hp_ref[pl.ds(j, tm, stride=slab_rows), :] = words[:, j * LANES:(j + 1) * LANES]

    h2_hi = h2.astype(bf16)
    h2_lo = (h2 - h2_hi.astype(f32)).astype(bf16)
    both = jnp.dot(h2_hi, wrc_ref[...], preferred_element_type=f32)
    logits = (both[:, :LANES] + both[:, LANES:]
              + jnp.dot(h2_lo, wrh_ref[...], preferred_element_type=f32)
              + br_ref[...])

    lane = lax.broadcasted_iota(jnp.int32, (tm, LANES), 1)
    lane_f = lane.astype(f32)
    vals, idxs = [], []
    cur = logits
    for _ in range(TOP_K):
        m = jnp.max(cur, axis=-1, keepdims=True)
        ix = jnp.min(jnp.where(cur == m, lane_f, float(LANES)), axis=-1,
                     keepdims=True).astype(jnp.int32)
        vals.append(m)
        idxs.append(ix)
        cur = jnp.where(lane == ix, -jnp.inf, cur)
    exps = [jnp.exp(vk - vals[0]) for vk in vals]
    denom = exps[0]
    for ek in exps[1:]:
        denom = denom + ek
    inv = 1.0 / denom

    row = lax.broadcasted_iota(jnp.int32, (tm, tm), 0)
    col = lax.broadcasted_iota(jnp.int32, (tm, tm), 1)
    tri = jnp.where(col < row, 1.0, 0.0).astype(bf16)
    run = run_cnt[...]
    idx_out = jnp.zeros((tm, LANES), jnp.int32)
    wgt_out = jnp.zeros((tm, LANES), f32)
    rank_out = jnp.zeros((tm, LANES), f32)
    onehots = [jnp.where(lane == idxs[k], 1.0, 0.0) for k in range(TOP_K)]
    before_all = jnp.dot(tri, jnp.concatenate(onehots, axis=1).astype(bf16),
                         preferred_element_type=f32)
    for k in range(TOP_K):
        oh = onehots[k]
        before = before_all[:, k * LANES:(k + 1) * LANES]
        rank_k = jnp.sum(oh * (before + run), axis=-1, keepdims=True)
        run = run + jnp.sum(oh, axis=0, keepdims=True)
        idx_out = jnp.where(lane == k, idxs[k], idx_out)
        wgt_out = jnp.where(lane == k, exps[k] * inv, wgt_out)
        rank_out = jnp.where(lane == k, rank_k, rank_out)
    run = jnp.where(valid, run, run_cnt[...])
    run_cnt[...] = run
    idx_ref[...] = idx_out
    wgt_ref[...] = wgt_out
    rank_ref[...] = rank_out.astype(jnp.int32)
    cnt_ref[...] = jnp.broadcast_to(run, cnt_ref.shape).astype(jnp.int32)


def _mix(x2, mod3, seq, g_pre, g_post, g_ffn, w_in, conv_a_w, conv_b_w, conv_b_b, ln_g, ln_b,
         w_out, wr_cat, wr_hi, br):
    n, d = x2.shape
    wa = conv_a_w.shape[1]
    wb = conv_b_w.shape[1]
    tm = MIX_TM
    assert seq % tm == 0 and tm >= HALO_B and wa % LANES == 0 and wb % LANES == 0
    assert wa + wb == d and d % (2 * LANES) == 0
    tiles_per_seq = seq // tm
    n_tiles = n // tm
    slab_rows = d // (2 * LANES)
    cur = lambda i: jnp.minimum(i, n_tiles - 1)
    prev = lambda i: jnp.maximum(i - 1, 0)
    const = lambda shape: pl.BlockSpec(shape, lambda i: (0,) * len(shape))
    resident = lambda shape: pl.BlockSpec(shape, lambda i: (0,) * len(shape),
                                          pipeline_mode=pl.Buffered(1))
    out_block = lambda w: pl.BlockSpec((tm, w), lambda i: (prev(i), 0))
    mod_block = lambda tile: pl.BlockSpec((1, mod3.shape[1], d),
                                          lambda i: (tile(i) // tiles_per_seq, 0, 0))
    kernel = functools.partial(_mix_kernel, tiles_per_seq=tiles_per_seq, wa=wa, wb=wb)
    return pl.pallas_call(
        kernel,
        grid=(n_tiles + 1,),
        in_specs=[pl.BlockSpec((tm, d), lambda i: (cur(i), 0)),
                  pl.BlockSpec((tm, d), lambda i: (prev(i), 0)),
                  mod_block(cur), mod_block(prev),
                  const((1, d)), const((1, d)), const((1, d)),
                  resident(w_in.shape),
                  const(conv_a_w.shape), const(conv_b_w.shape),
                  const((1, wb)), const((1, wb)), const((1, wb)),
                  resident(w_out.shape),
                  const(wr_cat.shape), const(wr_hi.shape), const((1, LANES))],
        out_specs=[out_block(d),
                   pl.BlockSpec((tm * slab_rows, LANES), lambda i: (prev(i), 0)),
                   out_block(LANES), out_block(LANES), out_block(LANES),
                   const((SUBLANES, LANES))],
        out_shape=[jax.ShapeDtypeStruct((n, d), jnp.float32),
                   jax.ShapeDtypeStruct((n * slab_rows, LANES), jnp.uint32),
                   jax.ShapeDtypeStruct((n, LANES), jnp.int32),
                   jax.ShapeDtypeStruct((n, LANES), jnp.float32),
                   jax.ShapeDtypeStruct((n, LANES), jnp.int32),
                   jax.ShapeDtypeStruct((SUBLANES, LANES), jnp.int32)],
        scratch_shapes=[pltpu.VMEM((wa // LANES, tm + HALO_A, LANES), jnp.float32),
                        pltpu.VMEM((wb // LANES, tm + HALO_B, LANES), jnp.float32),
                        pltpu.VMEM((tm, wb), jnp.float32),
                        pltpu.VMEM((tm, d), jnp.bfloat16),
                        pltpu.VMEM((1, LANES), jnp.float32)],
        compiler_params=_cparams(("arbitrary",)),
        name="mix",
    )(x2, x2, mod3, mod3, g_pre, g_post, g_ffn, w_in, conv_a_w, conv_b_w, conv_b_b, ln_g, ln_b,
      w_out, wr_cat, wr_hi, br)


def _plan_kernel(idx_ref, rank_ref, start_ref, dest_ref):
    tm = idx_ref.shape[0]
    start = jnp.broadcast_to(start_ref[...], (tm, LANES))
    dest_ref[...] = jnp.take_along_axis(start, idx_ref[...], axis=1) + rank_ref[...]


def _plan(idx, rank, start_row):
    n = idx.shape[0]
    tm = min(PLAN_TM, n)
    assert n % tm == 0
    blk = pl.BlockSpec((tm, LANES), lambda i: (i, 0))
    return pl.pallas_call(
        _plan_kernel,
        grid=(n // tm,),
        in_specs=[blk, blk, pl.BlockSpec((1, LANES), lambda i: (0, 0))],
        out_specs=blk,
        out_shape=jax.ShapeDtypeStruct((n, LANES), jnp.int32),
        compiler_params=_cparams(("arbitrary",)),
        name="plan",
    )(idx, rank, start_row)


def _slab(ref, row, slab_rows, count=1):
    return ref.at[pl.ds(pl.multiple_of(row * slab_rows, slab_rows), count * slab_rows)]


def _dispatch_kernel(zstart_ref, tail_ref, dest_ref, hp_ref, xs_ref, zbuf, sem, zsem,
                     *, tm, n_experts, slab_rows):
    i = pl.program_id(0)

    def zero_copy(start):
        return pltpu.make_async_copy(zbuf, _slab(xs_ref, start, slab_rows, SUB), zsem)

    @pl.when(i == 0)
    def _():
        zbuf[...] = jnp.zeros_like(zbuf)
        for parity in range(2):
            for e in range(parity, n_experts, 2):
                zero_copy(zstart_ref[e]).start()
            for e in range(parity, n_experts, 2):
                zero_copy(zstart_ref[e]).wait()

        def tail_start(t, carry):
            zero_copy(tail_ref[0] + t * SUB).start()
            return carry

        def tail_wait(t, carry):
            zero_copy(tail_ref[0] + t * SUB).wait()
            return carry
        lax.fori_loop(0, tail_ref[1], tail_start, 0)
        lax.fori_loop(0, tail_ref[1], tail_wait, 0)

    def body(t, carry):
        for u in range(DISPATCH_UNROLL):
            tok = t * DISPATCH_UNROLL + u
            src = _slab(hp_ref, tok, slab_rows)
            for k in range(TOP_K):
                dst = _slab(xs_ref, dest_ref[0, 0, tok * TOP_K + k], slab_rows)
                pltpu.make_async_copy(src, dst, sem).start(priority=k % 2)
        return carry
    lax.fori_loop(0, tm // DISPATCH_UNROLL, body, 0)
    for k in range(TOP_K):
        pltpu.make_async_copy(hp_ref, _slab(xs_ref, 0, slab_rows, tm), sem).wait()


def _dispatch(hp, dest_flat, zstart, tail, p_rows, slab_rows):
    n = hp.shape[0] // slab_rows
    tm = min(DISPATCH_TM, n)
    assert n % tm == 0 and tm % DISPATCH_UNROLL == 0
    n_experts = zstart.shape[0]
    dest3 = dest_flat.reshape(n // tm, 1, tm * TOP_K)
    kernel = functools.partial(_dispatch_kernel, tm=tm, n_experts=n_experts, slab_rows=slab_rows)
    return pl.pallas_call(
        kernel,
        grid_spec=pltpu.PrefetchScalarGridSpec(
            num_scalar_prefetch=2,
            grid=(n // tm,),
            in_specs=[pl.BlockSpec((1, 1, tm * TOP_K), lambda i, *_: (i, 0, 0),
                                   memory_space=pltpu.SMEM),
                      pl.BlockSpec((tm * slab_rows, LANES), lambda i, *_: (i, 0))],
            out_specs=pl.BlockSpec(memory_space=pl.ANY),
            scratch_shapes=[pltpu.VMEM((SUB * slab_rows, LANES), jnp.uint32),
                            pltpu.SemaphoreType.DMA(()),
                            pltpu.SemaphoreType.DMA(())]),
        out_shape=jax.ShapeDtypeStruct((p_rows * slab_rows, LANES), jnp.uint32),
        compiler_params=_cparams(("arbitrary",)),
        name="dispatch",
    )(zstart, tail, dest3, hp)


def _expert_kernel(vexp_ref, vrow_ref, vnsub_ref, tail_ref,
                   xs_ref, wg_ref, wu_ref, wd_ref, bg_ref, bu_ref, bd_ref, ys_ref,
                   xb, acc, xstage, ostage, xsem, osem, zsem, *, n_f):
    del vexp_ref
    v = pl.program_id(0)
    j = pl.program_id(1)
    d = xb.shape[1]
    slab_rows = d // (2 * LANES)
    stage_rows = SUB * slab_rows
    f32, bf16 = jnp.float32, jnp.bfloat16
    nsub = vnsub_ref[v]
    row0 = vrow_ref[v]

    def stage_slot(ref, slot):
        return ref.at[pl.ds(pl.multiple_of(slot * stage_rows, stage_rows), stage_rows)]

    def x_copy(first_row, s):
        slot = s % X_SLOTS
        return pltpu.make_async_copy(_slab(xs_ref, first_row + s * SUB, slab_rows, SUB),
                                     stage_slot(xstage, slot), xsem.at[slot])

    def x_prefetch(first_row, count):
        for s in range(X_AHEAD):
            @pl.when(s < count)
            def _():
                x_copy(first_row, s).start()

    def o_copy(s, slot):
        return pltpu.make_async_copy(stage_slot(ostage, slot),
                                     _slab(ys_ref, row0 + s * SUB, slab_rows, SUB), osem.at[slot])

    @pl.when((v == 0) & (j == 0))
    def _():
        ostage[0:stage_rows, :] = jnp.zeros((stage_rows, LANES), jnp.uint32)

        def z_copy(t):
            return pltpu.make_async_copy(stage_slot(ostage, 0),
                                         _slab(ys_ref, tail_ref[0] + t * SUB, slab_rows, SUB), zsem)

        def z_start(t, carry):
            z_copy(t).start()
            return carry

        def z_wait(t, carry):
            z_copy(t).wait()
            return carry
        lax.fori_loop(0, tail_ref[1], z_start, 0)
        lax.fori_loop(0, tail_ref[1], z_wait, 0)
        x_prefetch(row0, nsub)

    @pl.when(j == 0)
    def _():
        def body(s, carry):
            slot = s % X_SLOTS

            @pl.when(s + X_AHEAD < nsub)
            def _():
                x_copy(row0, s + X_AHEAD).start()
            x_copy(row0, s).wait()
            base = pl.multiple_of(s * SUB, SUB)
            for jj in range(slab_rows):
                lo, hi = _unpack_slab_words(
                    xstage[pl.ds(slot * stage_rows + jj, SUB, stride=slab_rows), :])
                xb[pl.ds(base, SUB), jj * LANES:(jj + 1) * LANES] = lo.astype(bf16)
                xb[pl.ds(base, SUB), d // 2 + jj * LANES:d // 2 + (jj + 1) * LANES] = hi.astype(bf16)
            acc[pl.ds(base, SUB), :] = jnp.zeros((SUB, d), f32)
            return carry
        lax.fori_loop(0, nsub, body, 0)

    def chain(base, m):
        wg = wg_ref[...].astype(bf16)
        wu = wu_ref[...].astype(bf16)
        wd = wd_ref[...].astype(bf16)
        piece = min(m, CHAIN_PIECE * SUB)
        for p in range(m // piece):
            rows = pl.ds(pl.multiple_of(base + p * piece, piece), piece)
            xt = xb[rows, :]
            g = jnp.dot(xt, wg, preferred_element_type=f32) + bg_ref[...]
            u = jnp.dot(xt, wu, preferred_element_type=f32) + bu_ref[...]
            g = jnp.minimum(g, SWIGLU_LIMIT)
            u = jnp.clip(u, -SWIGLU_LIMIT, SWIGLU_LIMIT)
            a = (u + 1.0) * (g * jax.nn.sigmoid(SWIGLU_ALPHA * g))
            acc[rows, :] += jnp.dot(a.astype(bf16), wd, preferred_element_type=f32)

    def big_body(q, carry):
        chain(pl.multiple_of(q * (CHAIN_MAX * SUB), CHAIN_MAX * SUB), CHAIN_MAX * SUB)
        return carry
    lax.fori_loop(0, nsub // CHAIN_MAX, big_body, 0)
    units = CHAIN_MAX // 2
    while units >= 1:
        @pl.when((nsub & units) != 0)
        def _(units=units):
            start = (nsub // (2 * units)) * (2 * units)
            chain(pl.multiple_of(start * SUB, units * SUB), units * SUB)
        units //= 2

    @pl.when((j == n_f - 1) & (v + 1 < pl.num_programs(0)))
    def _():
        nxt = jnp.minimum(v + 1, pl.num_programs(0) - 1)
        x_prefetch(vrow_ref[nxt], vnsub_ref[nxt])

    @pl.when((j == n_f - 1) & (nsub > 0))
    def _():
        def body(s, carry):
            slot = s % O_SLOTS

            @pl.when(s >= O_SLOTS)
            def _():
                o_copy(s - O_SLOTS, slot).wait()
            words = _pack_slab_words(acc[pl.ds(pl.multiple_of(s * SUB, SUB), SUB), :] + bd_ref[...])
            for jj in range(slab_rows):
                ostage[pl.ds(slot * stage_rows + jj, SUB, stride=slab_rows), :] = (
                    words[:, jj * LANES:(jj + 1) * LANES])
            o_copy(s, slot).start()
            return carry
        lax.fori_loop(0, nsub, body, 0)

        def drain(s, carry):
            o_copy(s, s % O_SLOTS).wait()
            return carry
        lax.fori_loop(jnp.maximum(nsub - O_SLOTS, 0), nsub, drain, 0)


def _expert(xs, w_gate, b_gate, w_up, b_up, w_down, b_down, vexp, vrow, vnsub, tail, p_rows):
    n_experts, d, f = w_gate.shape
    tf = min(EXPERT_TF, f)
    assert f % tf == 0 and EXPERT_ROWS % SUB == 0
    n_f = f // tf
    slab_rows = d // (2 * LANES)
    n_visits = vexp.shape[0]

    def w_idx(v, j, ve, vr, vn, tl):
        return jnp.where(vn[v] == 0, n_f - 1, j)

    kernel = functools.partial(_expert_kernel, n_f=n_f)
    return pl.pallas_call(
        kernel,
        grid_spec=pltpu.PrefetchScalarGridSpec(
            num_scalar_prefetch=4,
            grid=(n_visits, n_f),
            in_specs=[
                pl.BlockSpec(memory_space=pl.ANY),
                pl.BlockSpec((None, d, tf), lambda v, j, ve, *s: (ve[v], 0, w_idx(v, j, ve, *s))),
                pl.BlockSpec((None, d, tf), lambda v, j, ve, *s: (ve[v], 0, w_idx(v, j, ve, *s))),
                pl.BlockSpec((None, tf, d), lambda v, j, ve, *s: (ve[v], w_idx(v, j, ve, *s), 0)),
                pl.BlockSpec((None, 1, tf), lambda v, j, ve, *s: (ve[v], 0, w_idx(v, j, ve, *s))),
                pl.BlockSpec((None, 1, tf), lambda v, j, ve, *s: (ve[v], 0, w_idx(v, j, ve, *s))),
                pl.BlockSpec((None, 1, d), lambda v, j, ve, *s: (ve[v], 0, 0)),
            ],
            out_specs=pl.BlockSpec(memory_space=pl.ANY),
            scratch_shapes=[pltpu.VMEM((EXPERT_ROWS, d), jnp.bfloat16),
                            pltpu.VMEM((EXPERT_ROWS, d), jnp.float32),
                            pltpu.VMEM((X_SLOTS * SUB * slab_rows, LANES), jnp.uint32),
                            pltpu.VMEM((O_SLOTS * SUB * slab_rows, LANES), jnp.uint32),
                            pltpu.SemaphoreType.DMA((X_SLOTS,)),
                            pltpu.SemaphoreType.DMA((O_SLOTS,)),
                            pltpu.SemaphoreType.DMA(())]),
        out_shape=jax.ShapeDtypeStruct((p_rows * slab_rows, LANES), jnp.uint32),
        compiler_params=_cparams(("arbitrary", "arbitrary")),
        name="expert",
    )(vexp, vrow, vnsub, tail, xs, w_gate, w_up, w_down,
      b_gate.reshape(n_experts, 1, f), b_up.reshape(n_experts, 1, f),
      b_down.reshape(n_experts, 1, d))


def _combine_kernel(dest_ref, dnext_ref, ys_ref, wgt_ref, x1_ref, mod_ref, g_ref, o_ref,
                    buf, ybuf, sem, *, tm):
    i = pl.program_id(0)
    d = x1_ref.shape[1]
    slab_rows = d // (2 * LANES)
    half = TOP_K * tm

    def gather(idx_ref, slot):
        def body(t, carry):
            for u in range(COMBINE_UNROLL):
                tok = t * COMBINE_UNROLL + u
                for k in range(TOP_K):
                    src = _slab(ys_ref, idx_ref[0, 0, tok * TOP_K + k], slab_rows)
                    dst = _slab(buf, slot * half + k * tm + tok, slab_rows)
                    pltpu.make_async_copy(src, dst, sem.at[slot]).start(priority=k % 2)
            return carry
        lax.fori_loop(0, tm // COMBINE_UNROLL, body, 0)

    slot = i % 2

    @pl.when(i == 0)
    def _():
        gather(dest_ref, 0)

    @pl.when(i + 1 < pl.num_programs(0))
    def _():
        gather(dnext_ref, 1 - slot)

    pltpu.make_async_copy(_slab(ys_ref, 0, slab_rows, half), _slab(buf, slot * half, slab_rows, half),
                          sem.at[slot]).wait()

    wgt = wgt_ref[...]
    base = slot * half * slab_rows
    for jj in range(slab_rows):
        ylo = jnp.zeros((tm, LANES), jnp.float32)
        yhi = jnp.zeros((tm, LANES), jnp.float32)
        for k in range(TOP_K):
            lo, hi = _unpack_slab_words(
                buf[pl.ds(base + k * tm * slab_rows + jj, tm, stride=slab_rows), :])
            ylo = ylo + wgt[:, k:k + 1] * lo
            yhi = yhi + wgt[:, k:k + 1] * hi
        ybuf[:, jj * LANES:(jj + 1) * LANES] = ylo
        ybuf[:, d // 2 + jj * LANES:d // 2 + (jj + 1) * LANES] = yhi
    mod = mod_ref[0]
    o_ref[...] = x1_ref[...] + mod[5:6] * _rms(ybuf[...], g_ref[...])


def _combine(ys, dest_flat, wgt, x1, mod3, seq, g_post):
    n, d = x1.shape
    slab_rows = d // (2 * LANES)
    tm = COMBINE_TM
    assert seq % tm == 0
    tiles_per_seq = seq // tm
    n_tiles = n // tm
    dest3 = dest_flat.reshape(n_tiles, 1, tm * TOP_K)
    kernel = functools.partial(_combine_kernel, tm=tm)
    return pl.pallas_call(
        kernel,
        grid=(n_tiles,),
        in_specs=[pl.BlockSpec((1, 1, tm * TOP_K), lambda i: (i, 0, 0), memory_space=pltpu.SMEM),
                  pl.BlockSpec((1, 1, tm * TOP_K), lambda i: (jnp.minimum(i + 1, n_tiles - 1), 0, 0),
                               memory_space=pltpu.SMEM),
                  pl.BlockSpec(memory_space=pl.ANY),
                  pl.BlockSpec((tm, LANES), lambda i: (i, 0)),
                  pl.BlockSpec((tm, d), lambda i: (i, 0)),
                  pl.BlockSpec((1, mod3.shape[1], d), lambda i: (i // tiles_per_seq, 0, 0)),
                  pl.BlockSpec((1, d), lambda i: (0, 0))],
        out_specs=pl.BlockSpec((tm, d), lambda i: (i, 0)),
        out_shape=jax.ShapeDtypeStruct((n, d), jnp.float32),
        scratch_shapes=[pltpu.VMEM((2 * TOP_K * tm * slab_rows, LANES), jnp.uint32),
                        pltpu.VMEM((tm, d), jnp.float32),
                        pltpu.SemaphoreType.DMA((2,))],
        compiler_params=_cparams(("arbitrary",)),
        name="combine",
    )(dest3, dest3, ys, wgt, x1, mod3, g_post)


def _routing_tables(counts, n_assign):
    n_experts = counts.shape[0]
    p_rows = n_assign + n_experts * SUB
    n_visits = n_experts + -(-p_rows // EXPERT_ROWS)

    padded = jnp.maximum((counts + SUB - 1) // SUB, 1) * SUB
    e_ids = jnp.arange(n_experts, dtype=jnp.int32)
    lower = e_ids[None, :] <= e_ids[:, None]
    pad_end = jnp.sum(jnp.where(lower, padded[None, :], 0), axis=1)
    pad_start = pad_end - padded
    total = pad_end[-1]
    zstart = (pad_start + counts).astype(jnp.int32)
    tail_xs = jnp.stack([total, (p_rows + SUB - total) // SUB]).astype(jnp.int32)
    tail_ys = jnp.stack([total, (p_rows - total) // SUB]).astype(jnp.int32)

    n_chunk = (padded + EXPERT_ROWS - 1) // EXPERT_ROWS
    chunk_end = jnp.sum(jnp.where(lower, n_chunk[None, :], 0), axis=1)
    visit = jnp.arange(n_visits, dtype=jnp.int32)
    used = visit < chunk_end[-1]
    vexp = jnp.minimum(jnp.sum((chunk_end[None, :] <= visit[:, None]).astype(jnp.int32), axis=1),
                       n_experts - 1)
    onehot = vexp[:, None] == e_ids[None, :]
    pick = lambda a: jnp.sum(jnp.where(onehot, a[None, :], 0), axis=1)
    chunk = visit - pick(chunk_end - n_chunk)
    vrow = jnp.where(used, pick(pad_start) + chunk * EXPERT_ROWS, 0)
    vnsub = jnp.where(used, jnp.minimum(pick(padded) - chunk * EXPERT_ROWS, EXPERT_ROWS) // SUB, 0)
    return (pad_start.astype(jnp.int32), zstart, tail_xs, tail_ys, vexp.astype(jnp.int32),
            vrow.astype(jnp.int32), vnsub.astype(jnp.int32), p_rows)


def kernel(x, c, w_ada, b_ada, g_pre_mix, g_post_mix, w_in, conv_a_w, conv_b_w, conv_b_b, ln_b_g, ln_b_b, w_out, g_pre_ffn, g_post_ffn, w_router, b_router, w_gate, b_gate, w_up, b_up, w_down, b_down):
    bsz, seq, d = x.shape
    n = bsz * seq
    depth = w_ada.shape[0]
    n_experts = w_router.shape[-1]
    assert n_experts <= LANES
    xf = x.reshape(n, d)
    for l in range(depth):
        mod = _ada(c, w_ada[l], b_ada[l])
        n_mod = mod.shape[1] // d
        mod3 = mod.reshape(bsz, n_mod, d)

        wr = jnp.zeros((d, LANES), jnp.float32).at[:, :n_experts].set(w_router[l])
        wr_hi = wr.astype(jnp.bfloat16)
        wr_lo = (wr - wr_hi.astype(jnp.float32)).astype(jnp.bfloat16)
        br = jnp.full((1, LANES), NEG_BIG, jnp.float32).at[0, :n_experts].set(b_router[l])

        x1, hp, idx, wgt, rank, cnt = _mix(
            xf, mod3, seq, g_pre_mix[l].reshape(1, d), g_post_mix[l].reshape(1, d),
            g_pre_ffn[l].reshape(1, d), w_in[l].astype(jnp.bfloat16), conv_a_w[l], conv_b_w[l],
            conv_b_b[l].reshape(1, -1), ln_b_g[l].reshape(1, -1), ln_b_b[l].reshape(1, -1),
            w_out[l].astype(jnp.bfloat16), jnp.concatenate([wr_hi, wr_lo], axis=1), wr_hi, br)

        (pad_start, zstart, tail_xs, tail_ys, vexp, vrow, vnsub,
         p_rows) = _routing_tables(cnt[0, :n_experts], n * TOP_K)
        start_row = jnp.zeros((1, LANES), jnp.int32).at[0, :n_experts].set(pad_start)
        dest = _plan(idx, rank, start_row)
        dest_flat = dest[:, :TOP_K].reshape(n * TOP_K)

        slab_rows = d // (2 * LANES)
        xs = _dispatch(hp, dest_flat, zstart, tail_xs, p_rows + SUB, slab_rows)
        ys = _expert(xs, w_gate[l], b_gate[l], w_up[l], b_up[l], w_down[l], b_down[l],
                     vexp, vrow, vnsub, tail_ys, p_rows)
        xf = _combine(ys, dest_flat, wgt, x1, mod3, seq, g_post_ffn[l].reshape(1, d))
    return xf.reshape(bsz, seq, d)
```

```python
import functools

import jax
import jax.numpy as jnp
from jax import lax
from jax.experimental import pallas as pl
from jax.experimental.pallas import tpu as pltpu

EPS = 1e-6
TOP_K = 4
CONV_A = 3
CONV_B = 31
SWIGLU_ALPHA = 1.702
SWIGLU_LIMIT = 7.0

LANES = 128
SUBLANES = 8
VMEM_LIMIT_BYTES = 62 * 1024 * 1024

ADA_TN = 1024
MIX_TM = 256
HALO_A = 8
HALO_B = 32
PLAN_TM = 2048
DISPATCH_TM = 1024
DISPATCH_UNROLL = 8
SUB = 128
EXPERT_ROWS = 2176
EXPERT_TF = 512
CHAIN_MAX = 8
CHAIN_PIECE = 8
X_AHEAD = 5
X_SLOTS = X_AHEAD + 1
O_SLOTS = 4
COMBINE_TM = 256
COMBINE_UNROLL = 8
NEG_BIG = -1e30


def _cparams(sem):
    return pltpu.CompilerParams(dimension_semantics=sem, vmem_limit_bytes=VMEM_LIMIT_BYTES)


def _rms(x, g):
    return x * lax.rsqrt(jnp.mean(x * x, axis=-1, keepdims=True) + EPS) * g


def _pack_slab_words(v):
    half = v.shape[1] // 2
    lo = pltpu.bitcast(v[:, :half].astype(jnp.bfloat16).astype(jnp.float32), jnp.uint32)
    hi = pltpu.bitcast(v[:, half:].astype(jnp.bfloat16).astype(jnp.float32), jnp.uint32)
    return (lo >> 16) | (hi & jnp.uint32(0xFFFF0000))


def _unpack_slab_words(w):
    lo = pltpu.bitcast(w << 16, jnp.float32)
    hi = pltpu.bitcast(w & jnp.uint32(0xFFFF0000), jnp.float32)
    return lo, hi


def _ada_kernel(c_ref, w_ref, b_ref, o_ref):
    c = c_ref[...]
    ca = (c * jax.nn.sigmoid(c)).astype(jnp.bfloat16)
    o_ref[...] = jnp.dot(ca, w_ref[...].astype(jnp.bfloat16),
                         preferred_element_type=jnp.float32) + b_ref[...]


def _ada(c, w_ada, b_ada):
    bsz, d = c.shape
    n_out = w_ada.shape[1]
    tn = min(ADA_TN, n_out)
    assert n_out % tn == 0 and bsz <= SUBLANES
    c8 = jnp.zeros((SUBLANES, d), jnp.float32).at[:bsz].set(c)
    out = pl.pallas_call(
        _ada_kernel,
        grid=(n_out // tn,),
        in_specs=[pl.BlockSpec((SUBLANES, d), lambda j: (0, 0)),
                  pl.BlockSpec((d, tn), lambda j: (0, j)),
                  pl.BlockSpec((1, tn), lambda j: (0, j))],
        out_specs=pl.BlockSpec((SUBLANES, tn), lambda j: (0, j)),
        out_shape=jax.ShapeDtypeStruct((SUBLANES, n_out), jnp.float32),
        compiler_params=_cparams(("arbitrary",)),
        name="ada",
    )(c8, w_ada, b_ada.reshape(1, n_out))
    return out[:bsz]


def _mix_kernel(x_ref, xp_ref, mod_ref, modp_ref, gpre_ref, gpost_ref, gffn_ref, win_ref,
                caw_ref, cbw_ref, cbb_ref, lng_ref, lnb_ref, wout_ref, wrc_ref, wrh_ref, br_ref,
                x1_ref, hp_ref, idx_ref, wgt_ref, rank_ref, cnt_ref,
                pa_buf, u_buf, v_buf, ycat, run_cnt, *, tiles_per_seq, wa, wb):
    i = pl.program_id(0)
    tm, d = x_ref.shape
    f32, bf16 = jnp.float32, jnp.bfloat16

    @pl.when(i == 0)
    def _():
        run_cnt[...] = jnp.zeros_like(run_cnt)
        ycat[...] = jnp.zeros_like(ycat)

    @pl.when(i % tiles_per_seq == 0)
    def _():
        pa_buf[:, 0:HALO_A, :] = jnp.zeros((wa // LANES, HALO_A, LANES), f32)
        u_buf[:, 0:HALO_B, :] = jnp.zeros((wb // LANES, HALO_B, LANES), f32)

    y_prev = jnp.dot(ycat[...], wout_ref[...], preferred_element_type=f32)

    x = x_ref[...]
    mod = mod_ref[0]
    h = (_rms(x, gpre_ref[...]) * (1.0 + mod[1:2]) + mod[0:1]).astype(bf16)
    proj_b = jnp.dot(h, win_ref[:, 3 * wa:3 * wa + 2 * wb], preferred_element_type=f32)
    proj_a = jnp.dot(h, win_ref[:, 0:3 * wa], preferred_element_type=f32)

    for cb in range(wa // LANES):
        sl = slice(cb * LANES, (cb + 1) * LANES)
        pa = proj_a[:, sl] * proj_a[:, 2 * wa + cb * LANES:2 * wa + (cb + 1) * LANES]
        pa_buf[cb, HALO_A:HALO_A + tm, :] = pa
        conv = caw_ref[CONV_A - 1:CONV_A, sl] * pa
        for k in range(CONV_A - 1):
            off = HALO_A - (CONV_A - 1) + k
            conv = conv + caw_ref[k:k + 1, sl] * pa_buf[cb, off:off + tm, :]
        ycat[:, sl] = (proj_a[:, wa + cb * LANES:wa + (cb + 1) * LANES] * conv).astype(bf16)
        pa_buf[cb, 0:HALO_A, :] = pa_buf[cb, tm:tm + HALO_A, :]

    for cb in range(wb // LANES):
        sl = slice(cb * LANES, (cb + 1) * LANES)
        u = proj_b[:, sl] * jax.nn.sigmoid(proj_b[:, wb + cb * LANES:wb + (cb + 1) * LANES])
        u_buf[cb, HALO_B:HALO_B + tm, :] = u
        acc = cbb_ref[:, sl] + cbw_ref[CONV_B - 1:CONV_B, sl] * u
        for k in range(CONV_B - 1):
            off = HALO_B - (CONV_B - 1) + k
            acc = acc + cbw_ref[k:k + 1, sl] * u_buf[cb, off:off + tm, :]
        v_buf[:, sl] = acc
        u_buf[cb, 0:HALO_B, :] = u_buf[cb, tm:tm + HALO_B, :]
    v = v_buf[...]
    mu = jnp.mean(v, axis=-1, keepdims=True)
    vc = v - mu
    var = jnp.mean(vc * vc, axis=-1, keepdims=True)
    yb = vc * lax.rsqrt(var + EPS) * lng_ref[...] + lnb_ref[...]
    ycat[:, wa:wa + wb] = (yb * jax.nn.sigmoid(yb)).astype(bf16)

    _mix_stage2(i >= 1, y_prev, xp_ref, modp_ref, gpost_ref, gffn_ref, wrc_ref, wrh_ref, br_ref,
                x1_ref, hp_ref, idx_ref, wgt_ref, rank_ref, cnt_ref, run_cnt)


def _mix_stage2(valid, y, x_ref, mod_ref, gpost_ref, gffn_ref, wrc_ref, wrh_ref, br_ref,
                x1_ref, hp_ref, idx_ref, wgt_ref, rank_ref, cnt_ref, run_cnt):
    tm, d = x_ref.shape
    f32, bf16 = jnp.float32, jnp.bfloat16
    x = x_ref[...]
    mod = mod_ref[0]
    x1 = x + mod[2:3] * _rms(y, gpost_ref[...])
    x1_ref[...] = x1

    h2 = _rms(x1, gffn_ref[...]) * (1.0 + mod[4:5]) + mod[3:4]
    words = _pack_slab_words(h2)
    slab_rows = d // (2 * LANES)
    for j in range(slab_rows):
        hp_ref[pl.ds(j, tm, stride=slab_rows), :] = words[:, j * LANES:(j + 1) * LANES]

    h2_hi = h2.astype(bf16)
    h2_lo = (h2 - h2_hi.astype(f32)).astype(bf16)
    both = jnp.dot(h2_hi, wrc_ref[...], preferred_element_type=f32)
    logits = (both[:, :LANES] + both[:, LANES:]
              + jnp.dot(h2_lo, wrh_ref[...], preferred_element_type=f32)
              + br_ref[...])

    lane = lax.broadcasted_iota(jnp.int32, (tm, LANES), 1)
    lane_f = lane.astype(f32)
    vals, idxs = [], []
    cur = logits
    for _ in range(TOP_K):
        m = jnp.max(cur, axis=-1, keepdims=True)
        ix = jnp.min(jnp.where(cur == m, lane_f, float(LANES)), axis=-1,
                     keepdims=True).astype(jnp.int32)
        vals.append(m)
        idxs.append(ix)
        cur = jnp.where(lane == ix, -jnp.inf, cur)
    exps = [jnp.exp(vk - vals[0]) for vk in vals]
    denom = exps[0]
    for ek in exps[1:]:
        denom = denom + ek
    inv = 1.0 / denom

    row = lax.broadcasted_iota(jnp.int32, (tm, tm), 0)
    col = lax.broadcasted_iota(jnp.int32, (tm, tm), 1)
    tri = jnp.where(col < row, 1.0, 0.0).astype(bf16)
    run = run_cnt[...]
    idx_out = jnp.zeros((tm, LANES), jnp.int32)
    wgt_out = jnp.zeros((tm, LANES), f32)
    rank_out = jnp.zeros((tm, LANES), f32)
    onehots = [jnp.where(lane == idxs[k], 1.0, 0.0) for k in range(TOP_K)]
    before_all = jnp.dot(tri, jnp.concatenate(onehots, axis=1).astype(bf16),
                         preferred_element_type=f32)
    for k in range(TOP_K):
        oh = onehots[k]
        before = before_all[:, k * LANES:(k + 1) * LANES]
        rank_k = jnp.sum(oh * (before + run), axis=-1, keepdims=True)
        run = run + jnp.sum(oh, axis=0, keepdims=True)
        idx_out = jnp.where(lane == k, idxs[k], idx_out)
        wgt_out = jnp.where(lane == k, exps[k] * inv, wgt_out)
        rank_out = jnp.where(lane == k, rank_k, rank_out)
    run = jnp.where(valid, run, run_cnt[...])
    run_cnt[...] = run
    idx_ref[...] = idx_out
    wgt_ref[...] = wgt_out
    rank_ref[...] = rank_out.astype(jnp.int32)
    cnt_ref[...] = jnp.broadcast_to(run, cnt_ref.shape).astype(jnp.int32)


def _mix(x2, mod3, seq, g_pre, g_post, g_ffn, w_in, conv_a_w, conv_b_w, conv_b_b, ln_g, ln_b,
         w_out, wr_cat, wr_hi, br):
    n, d = x2.shape
    wa = conv_a_w.shape[1]
    wb = conv_b_w.shape[1]
    tm = MIX_TM
    assert seq % tm == 0 and tm >= HALO_B and wa % LANES == 0 and wb % LANES == 0
    assert wa + wb == d and d % (2 * LANES) == 0
    tiles_per_seq = seq // tm
    n_tiles = n // tm
    slab_rows = d // (2 * LANES)
    cur = lambda i: jnp.minimum(i, n_tiles - 1)
    prev = lambda i: jnp.maximum(i - 1, 0)
    const = lambda shape: pl.BlockSpec(shape, lambda i: (0,) * len(shape))
    resident = lambda shape: pl.BlockSpec(shape, lambda i: (0,) * len(shape),
                                          pipeline_mode=pl.Buffered(1))
    out_block = lambda w: pl.BlockSpec((tm, w), lambda i: (prev(i), 0))
    mod_block = lambda tile: pl.BlockSpec((1, mod3.shape[1], d),
                                          lambda i: (tile(i) // tiles_per_seq, 0, 0))
    kernel = functools.partial(_mix_kernel, tiles_per_seq=tiles_per_seq, wa=wa, wb=wb)
    return pl.pallas_call(
        kernel,
        grid=(n_tiles + 1,),
        in_specs=[pl.BlockSpec((tm, d), lambda i: (cur(i), 0)),
                  pl.BlockSpec((tm, d), lambda i: (prev(i), 0)),
                  mod_block(cur), mod_block(prev),
                  const((1, d)), const((1, d)), const((1, d)),
                  resident(w_in.shape),
                  const(conv_a_w.shape), const(conv_b_w.shape),
                  const((1, wb)), const((1, wb)), const((1, wb)),
                  resident(w_out.shape),
                  const(wr_cat.shape), const(wr_hi.shape), const((1, LANES))],
        out_specs=[out_block(d),
                   pl.BlockSpec((tm * slab_rows, LANES), lambda i: (prev(i), 0)),
                   out_block(LANES), out_block(LANES), out_block(LANES),
                   const((SUBLANES, LANES))],
        out_shape=[jax.ShapeDtypeStruct((n, d), jnp.float32),
                   jax.ShapeDtypeStruct((n * slab_rows, LANES), jnp.uint32),
                   jax.ShapeDtypeStruct((n, LANES), jnp.int32),
                   jax.ShapeDtypeStruct((n, LANES), jnp.float32),
                   jax.ShapeDtypeStruct((n, LANES), jnp.int32),
                   jax.ShapeDtypeStruct((SUBLANES, LANES), jnp.int32)],
        scratch_shapes=[pltpu.VMEM((wa // LANES, tm + HALO_A, LANES), jnp.float32),
                        pltpu.VMEM((wb // LANES, tm + HALO_B, LANES), jnp.float32),
                        pltpu.VMEM((tm, wb), jnp.float32),
                        pltpu.VMEM((tm, d), jnp.bfloat16),
                        pltpu.VMEM((1, LANES), jnp.float32)],
        compiler_params=_cparams(("arbitrary",)),
        name="mix",
    )(x2, x2, mod3, mod3, g_pre, g_post, g_ffn, w_in, conv_a_w, conv_b_w, conv_b_b, ln_g, ln_b,
      w_out, wr_cat, wr_hi, br)


def _plan_kernel(idx_ref, rank_ref, start_ref, dest_ref):
    tm = idx_ref.shape[0]
    start = jnp.broadcast_to(start_ref[...], (tm, LANES))
    dest_ref[...] = jnp.take_along_axis(start, idx_ref[...], axis=1) + rank_ref[...]


def _plan(idx, rank, start_row):
    n = idx.shape[0]
    tm = min(PLAN_TM, n)
    assert n % tm == 0
    blk = pl.BlockSpec((tm, LANES), lambda i: (i, 0))
    return pl.pallas_call(
        _plan_kernel,
        grid=(n // tm,),
        in_specs=[blk, blk, pl.BlockSpec((1, LANES), lambda i: (0, 0))],
        out_specs=blk,
        out_shape=jax.ShapeDtypeStruct((n, LANES), jnp.int32),
        compiler_params=_cparams(("arbitrary",)),
        name="plan",
    )(idx, rank, start_row)


def _slab(ref, row, slab_rows, count=1):
    return ref.at[pl.ds(pl.multiple_of(row * slab_rows, slab_rows), count * slab_rows)]


def _dispatch_kernel(zstart_ref, tail_ref, dest_ref, hp_ref, xs_ref, zbuf, sem, zsem,
                     *, tm, n_experts, slab_rows):
    i = pl.program_id(0)

    def zero_copy(start):
        return pltpu.make_async_copy(zbuf, _slab(xs_ref, start, slab_rows, SUB), zsem)

    @pl.when(i == 0)
    def _():
        zbuf[...] = jnp.zeros_like(zbuf)
        for parity in range(2):
            for e in range(parity, n_experts, 2):
                zero_copy(zstart_ref[e]).start()
            for e in range(parity, n_experts, 2):
                zero_copy(zstart_ref[e]).wait()

        def tail_start(t, carry):
            zero_copy(tail_ref[0] + t * SUB).start()
            return carry

        def tail_wait(t, carry):
            zero_copy(tail_ref[0] + t * SUB).wait()
            return carry
        lax.fori_loop(0, tail_ref[1], tail_start, 0)
        lax.fori_loop(0, tail_ref[1], tail_wait, 0)

    def body(t, carry):
        for u in range(DISPATCH_UNROLL):
            tok = t * DISPATCH_UNROLL + u
            src = _slab(hp_ref, tok, slab_rows)
            for k in range(TOP_K):
                dst = _slab(xs_ref, dest_ref[0, 0, tok * TOP_K + k], slab_rows)
                pltpu.make_async_copy(src, dst, sem).start(priority=k % 2)
        return carry
    lax.fori_loop(0, tm // DISPATCH_UNROLL, body, 0)
    for k in range(TOP_K):
        pltpu.make_async_copy(hp_ref, _slab(xs_ref, 0, slab_rows, tm), sem).wait()


def _dispatch(hp, dest_flat, zstart, tail, p_rows, slab_rows):
    n = hp.shape[0] // slab_rows
    tm = min(DISPATCH_TM, n)
    assert n % tm == 0 and tm % DISPATCH_UNROLL == 0
    n_experts = zstart.shape[0]
    dest3 = dest_flat.reshape(n // tm, 1, tm * TOP_K)
    kernel = functools.partial(_dispatch_kernel, tm=tm, n_experts=n_experts, slab_rows=slab_rows)
    return pl.pallas_call(
        kernel,
        grid_spec=pltpu.PrefetchScalarGridSpec(
            num_scalar_prefetch=2,
            grid=(n // tm,),
            in_specs=[pl.BlockSpec((1, 1, tm * TOP_K), lambda i, *_: (i, 0, 0),
                                   memory_space=pltpu.SMEM),
                      pl.BlockSpec((tm * slab_rows, LANES), lambda i, *_: (i, 0))],
            out_specs=pl.BlockSpec(memory_space=pl.ANY),
            scratch_shapes=[pltpu.VMEM((SUB * slab_rows, LANES), jnp.uint32),
                            pltpu.SemaphoreType.DMA(()),
                            pltpu.SemaphoreType.DMA(())]),
        out_shape=jax.ShapeDtypeStruct((p_rows * slab_rows, LANES), jnp.uint32),
        compiler_params=_cparams(("arbitrary",)),
        name="dispatch",
    )(zstart, tail, dest3, hp)


def _expert_kernel(vexp_ref, vrow_ref, vnsub_ref, tail_ref,
                   xs_ref, wg_ref, wu_ref, wd_ref, bg_ref, bu_ref, bd_ref, ys_ref,
                   xb, acc, xstage, ostage, xsem, osem, zsem, *, n_f):
    del vexp_ref
    v = pl.program_id(0)
    j = pl.program_id(1)
    d = xb.shape[1]
    slab_rows = d // (2 * LANES)
    stage_rows = SUB * slab_rows
    f32, bf16 = jnp.float32, jnp.bfloat16
    nsub = vnsub_ref[v]
    row0 = vrow_ref[v]

    def stage_slot(ref, slot):
        return ref.at[pl.ds(pl.multiple_of(slot * stage_rows, stage_rows), stage_rows)]

    def x_copy(first_row, s):
        slot = s % X_SLOTS
        return pltpu.make_async_copy(_slab(xs_ref, first_row + s * SUB, slab_rows, SUB),
                                     stage_slot(xstage, slot), xsem.at[slot])

    def x_prefetch(first_row, count):
        for s in range(X_AHEAD):
            @pl.when(s < count)
            def _():
                x_copy(first_row, s).start()

    def o_copy(s, slot):
        return pltpu.make_async_copy(stage_slot(ostage, slot),
                                     _slab(ys_ref, row0 + s * SUB, slab_rows, SUB), osem.at[slot])

    @pl.when((v == 0) & (j == 0))
    def _():
        ostage[0:stage_rows, :] = jnp.zeros((stage_rows, LANES), jnp.uint32)

        def z_copy(t):
            return pltpu.make_async_copy(stage_slot(ostage, 0),
                                         _slab(ys_ref, tail_ref[0] + t * SUB, slab_rows, SUB), zsem)

        def z_start(t, carry):
            z_copy(t).start()
            return carry

        def z_wait(t, carry):
            z_copy(t).wait()
            return carry
        lax.fori_loop(0, tail_ref[1], z_start, 0)
        lax.fori_loop(0, tail_ref[1], z_wait, 0)
        x_prefetch(row0, nsub)

    @pl.when(j == 0)
    def _():
        def body(s, carry):
            slot = s % X_SLOTS

            @pl.when(s + X_AHEAD < nsub)
            def _():
                x_copy(row0, s + X_AHEAD).start()
            x_copy(row0, s).wait()
            base = pl.multiple_of(s * SUB, SUB)
            for jj in range(slab_rows):
                lo, hi = _unpack_slab_words(
                    xstage[pl.ds(slot * stage_rows + jj, SUB, stride=slab_rows), :])
                xb[pl.ds(base, SUB), jj * LANES:(jj + 1) * LANES] = lo.astype(bf16)
                xb[pl.ds(base, SUB), d // 2 + jj * LANES:d // 2 + (jj + 1) * LANES] = hi.astype(bf16)
            acc[pl.ds(base, SUB), :] = jnp.zeros((SUB, d), f32)
            return carry
        lax.fori_loop(0, nsub, body, 0)

    def chain(base, m):
        wg = wg_ref[...].astype(bf16)
        wu = wu_ref[...].astype(bf16)
        wd = wd_ref[...].astype(bf16)
        piece = min(m, CHAIN_PIECE * SUB)
        for p in range(m // piece):
            rows = pl.ds(pl.multiple_of(base + p * piece, piece), piece)
            xt = xb[rows, :]
            g = jnp.dot(xt, wg, preferred_element_type=f32) + bg_ref[...]
            u = jnp.dot(xt, wu, preferred_element_type=f32) + bu_ref[...]
            g = jnp.minimum(g, SWIGLU_LIMIT)
            u = jnp.clip(u, -SWIGLU_LIMIT, SWIGLU_LIMIT)
            a = (u + 1.0) * (g * jax.nn.sigmoid(SWIGLU_ALPHA * g))
            acc[rows, :] += jnp.dot(a.astype(bf16), wd, preferred_element_type=f32)

    def big_body(q, carry):
        chain(pl.multiple_of(q * (CHAIN_MAX * SUB), CHAIN_MAX * SUB), CHAIN_MAX * SUB)
        return carry
    lax.fori_loop(0, nsub // CHAIN_MAX, big_body, 0)
    units = CHAIN_MAX // 2
    while units >= 1:
        @pl.when((nsub & units) != 0)
        def _(units=units):
            start = (nsub // (2 * units)) * (2 * units)
            chain(pl.multiple_of(start * SUB, units * SUB), units * SUB)
        units //= 2

    @pl.when((j == n_f - 1) & (v + 1 < pl.num_programs(0)))
    def _():
        nxt = jnp.minimum(v + 1, pl.num_programs(0) - 1)
        x_prefetch(vrow_ref[nxt], vnsub_ref[nxt])

    @pl.when((j == n_f - 1) & (nsub > 0))
    def _():
        def body(s, carry):
            slot = s % O_SLOTS

            @pl.when(s >= O_SLOTS)
            def _():
                o_copy(s - O_SLOTS, slot).wait()
            words = _pack_slab_words(acc[pl.ds(pl.multiple_of(s * SUB, SUB), SUB), :] + bd_ref[...])
            for jj in range(slab_rows):
                ostage[pl.ds(slot * stage_rows + jj, SUB, stride=slab_rows), :] = (
                    words[:, jj * LANES:(jj + 1) * LANES])
            o_copy(s, slot).start()
            return carry
        lax.fori_loop(0, nsub, body, 0)

        def drain(s, carry):
            o_copy(s, s % O_SLOTS).wait()
            return carry
        lax.fori_loop(jnp.maximum(nsub - O_SLOTS, 0), nsub, drain, 0)


def _expert(xs, w_gate, b_gate, w_up, b_up, w_down, b_down, vexp, vrow, vnsub, tail, p_rows):
    n_experts, d, f = w_gate.shape
    tf = min(EXPERT_TF, f)
    assert f % tf == 0 and EXPERT_ROWS % SUB == 0
    n_f = f // tf
    slab_rows = d // (2 * LANES)
    n_visits = vexp.shape[0]

    def w_idx(v, j, ve, vr, vn, tl):
        return jnp.where(vn[v] == 0, n_f - 1, j)

    kernel = functools.partial(_expert_kernel, n_f=n_f)
    return pl.pallas_call(
        kernel,
        grid_spec=pltpu.PrefetchScalarGridSpec(
            num_scalar_prefetch=4,
            grid=(n_visits, n_f),
            in_specs=[
                pl.BlockSpec(memory_space=pl.ANY),
                pl.BlockSpec((None, d, tf), lambda v, j, ve, *s: (ve[v], 0, w_idx(v, j, ve, *s))),
                pl.BlockSpec((None, d, tf), lambda v, j, ve, *s: (ve[v], 0, w_idx(v, j, ve, *s))),
                pl.BlockSpec((None, tf, d), lambda v, j, ve, *s: (ve[v], w_idx(v, j, ve, *s), 0)),
                pl.BlockSpec((None, 1, tf), lambda v, j, ve, *s: (ve[v], 0, w_idx(v, j, ve, *s))),
                pl.BlockSpec((None, 1, tf), lambda v, j, ve, *s: (ve[v], 0, w_idx(v, j, ve, *s))),
                pl.BlockSpec((None, 1, d), lambda v, j, ve, *s: (ve[v], 0, 0)),
            ],
            out_specs=pl.BlockSpec(memory_space=pl.ANY),
            scratch_shapes=[pltpu.VMEM((EXPERT_ROWS, d), jnp.bfloat16),
                            pltpu.VMEM((EXPERT_ROWS, d), jnp.float32),
                            pltpu.VMEM((X_SLOTS * SUB * slab_rows, LANES), jnp.uint32),
                            pltpu.VMEM((O_SLOTS * SUB * slab_rows, LANES), jnp.uint32),
                            pltpu.SemaphoreType.DMA((X_SLOTS,)),
                            pltpu.SemaphoreType.DMA((O_SLOTS,)),
                            pltpu.SemaphoreType.DMA(())]),
        out_shape=jax.ShapeDtypeStruct((p_rows * slab_rows, LANES), jnp.uint32),
        compiler_params=_cparams(("arbitrary", "arbitrary")),
        name="expert",
    )(vexp, vrow, vnsub, tail, xs, w_gate, w_up, w_down,
      b_gate.reshape(n_experts, 1, f), b_up.reshape(n_experts, 1, f),
      b_down.reshape(n_experts, 1, d))


def _combine_kernel(dest_ref, dnext_ref, ys_ref, wgt_ref, x1_ref, mod_ref, g_ref, o_ref,
                    buf, ybuf, sem, *, tm):
    i = pl.program_id(0)
    d = x1_ref.shape[1]
    slab_rows = d // (2 * LANES)
    half = TOP_K * tm

    def gather(idx_ref, slot):
        def body(t, carry):
            for u in range(COMBINE_UNROLL):
                tok = t * COMBINE_UNROLL + u
                for k in range(TOP_K):
                    src = _slab(ys_ref, idx_ref[0, 0, tok * TOP_K + k], slab_rows)
                    dst = _slab(buf, slot * half + k * tm + tok, slab_rows)
                    pltpu.make_async_copy(src, dst, sem.at[slot]).start(priority=k % 2)
            return carry
        lax.fori_loop(0, tm // COMBINE_UNROLL, body, 0)

    slot = i % 2

    @pl.when(i == 0)
    def _():
        gather(dest_ref, 0)

    @pl.when(i + 1 < pl.num_programs(0))
    def _():
        gather(dnext_ref, 1 - slot)

    pltpu.make_async_copy(_slab(ys_ref, 0, slab_rows, half), _slab(buf, slot * half, slab_rows, half),
                          sem.at[slot]).wait()

    wgt = wgt_ref[...]
    base = slot * half * slab_rows
    for jj in range(slab_rows):
        ylo = jnp.zeros((tm, LANES), jnp.float32)
        yhi = jnp.zeros((tm, LANES), jnp.float32)
        for k in range(TOP_K):
            lo, hi = _unpack_slab_words(
                buf[pl.ds(base + k * tm * slab_rows + jj, tm, stride=slab_rows), :])
            ylo = ylo + wgt[:, k:k + 1] * lo
            yhi = yhi + wgt[:, k:k + 1] * hi
        ybuf[:, jj * LANES:(jj + 1) * LANES] = ylo
        ybuf[:, d // 2 + jj * LANES:d // 2 + (jj + 1) * LANES] = yhi
    mod = mod_ref[0]
    o_ref[...] = x1_ref[...] + mod[5:6] * _rms(ybuf[...], g_ref[...])


def _combine(ys, dest_flat, wgt, x1, mod3, seq, g_post):
    n, d = x1.shape
    slab_rows = d // (2 * LANES)
    tm = COMBINE_TM
    assert seq % tm == 0
    tiles_per_seq = seq // tm
    n_tiles = n // tm
    dest3 = dest_flat.reshape(n_tiles, 1, tm * TOP_K)
    kernel = functools.partial(_combine_kernel, tm=tm)
    return pl.pallas_call(
        kernel,
        grid=(n_tiles,),
        in_specs=[pl.BlockSpec((1, 1, tm * TOP_K), lambda i: (i, 0, 0), memory_space=pltpu.SMEM),
                  pl.BlockSpec((1, 1, tm * TOP_K), lambda i: (jnp.minimum(i + 1, n_tiles - 1), 0, 0),
                               memory_space=pltpu.SMEM),
                  pl.BlockSpec(memory_space=pl.ANY),
                  pl.BlockSpec((tm, LANES), lambda i: (i, 0)),
                  pl.BlockSpec((tm, d), lambda i: (i, 0)),
                  pl.BlockSpec((1, mod3.shape[1], d), lambda i: (i // tiles_per_seq, 0, 0)),
                  pl.BlockSpec((1, d), lambda i: (0, 0))],
        out_specs=pl.BlockSpec((tm, d), lambda i: (i, 0)),
        out_shape=jax.ShapeDtypeStruct((n, d), jnp.float32),
        scratch_shapes=[pltpu.VMEM((2 * TOP_K * tm * slab_rows, LANES), jnp.uint32),
                        pltpu.VMEM((tm, d), jnp.float32),
                        pltpu.SemaphoreType.DMA((2,))],
        compiler_params=_cparams(("arbitrary",)),
        name="combine",
    )(dest3, dest3, ys, wgt, x1, mod3, g_post)


def _routing_tables(counts, n_assign):
    n_experts = counts.shape[0]
    p_rows = n_assign + n_experts * SUB
    n_visits = n_experts + -(-p_rows // EXPERT_ROWS)

    padded = jnp.maximum((counts + SUB - 1) // SUB, 1) * SUB
    e_ids = jnp.arange(n_experts, dtype=jnp.int32)
    lower = e_ids[None, :] <= e_ids[:, None]
    pad_end = jnp.sum(jnp.where(lower, padded[None, :], 0), axis=1)
    pad_start = pad_end - padded
    total = pad_end[-1]
    zstart = (pad_start + counts).astype(jnp.int32)
    tail_xs = jnp.stack([total, (p_rows + SUB - total) // SUB]).astype(jnp.int32)
    tail_ys = jnp.stack([total, (p_rows - total) // SUB]).astype(jnp.int32)

    n_chunk = (padded + EXPERT_ROWS - 1) // EXPERT_ROWS
    chunk_end = jnp.sum(jnp.where(lower, n_chunk[None, :], 0), axis=1)
    visit = jnp.arange(n_visits, dtype=jnp.int32)
    used = visit < chunk_end[-1]
    vexp = jnp.minimum(jnp.sum((chunk_end[None, :] <= visit[:, None]).astype(jnp.int32), axis=1),
                       n_experts - 1)
    onehot = vexp[:, None] == e_ids[None, :]
    pick = lambda a: jnp.sum(jnp.where(onehot, a[None, :], 0), axis=1)
    chunk = visit - pick(chunk_end - n_chunk)
    vrow = jnp.where(used, pick(pad_start) + chunk * EXPERT_ROWS, 0)
    vnsub = jnp.where(used, jnp.minimum(pick(padded) - chunk * EXPERT_ROWS, EXPERT_ROWS) // SUB, 0)
    return (pad_start.astype(jnp.int32), zstart, tail_xs, tail_ys, vexp.astype(jnp.int32),
            vrow.astype(jnp.int32), vnsub.astype(jnp.int32), p_rows)


def kernel(x, c, w_ada, b_ada, g_pre_mix, g_post_mix, w_in, conv_a_w, conv_b_w, conv_b_b, ln_b_g, ln_b_b, w_out, g_pre_ffn, g_post_ffn, w_router, b_router, w_gate, b_gate, w_up, b_up, w_down, b_down):
    bsz, seq, d = x.shape
    n = bsz * seq
    depth = w_ada.shape[0]
    n_experts = w_router.shape[-1]
    assert n_experts <= LANES
    xf = x.reshape(n, d)
    for l in range(depth):
        mod = _ada(c, w_ada[l], b_ada[l])
        n_mod = mod.shape[1] // d
        mod3 = mod.reshape(bsz, n_mod, d)

        wr = jnp.zeros((d, LANES), jnp.float32).at[:, :n_experts].set(w_router[l])
        wr_hi = wr.astype(jnp.bfloat16)
        wr_lo = (wr - wr_hi.astype(jnp.float32)).astype(jnp.bfloat16)
        br = jnp.full((1, LANES), NEG_BIG, jnp.float32).at[0, :n_experts].set(b_router[l])

        x1, hp, idx, wgt, rank, cnt = _mix(
            xf, mod3, seq, g_pre_mix[l].reshape(1, d), g_post_mix[l].reshape(1, d),
            g_pre_ffn[l].reshape(1, d), w_in[l].astype(jnp.bfloat16), conv_a_w[l], conv_b_w[l],
            conv_b_b[l].reshape(1, -1), ln_b_g[l].reshape(1, -1), ln_b_b[l].reshape(1, -1),
            w_out[l].astype(jnp.bfloat16), jnp.concatenate([wr_hi, wr_lo], axis=1), wr_hi, br)

        (pad_start, zstart, tail_xs, tail_ys, vexp, vrow, vnsub,
         p_rows) = _routing_tables(cnt[0, :n_experts], n * TOP_K)
        start_row = jnp.zeros((1, LANES), jnp.int32).at[0, :n_experts].set(pad_start)
        dest = _plan(idx, rank, start_row)
        dest_flat = dest[:, :TOP_K].reshape(n * TOP_K)

        slab_rows = d // (2 * LANES)
        xs = _dispatch(hp, dest_flat, zstart, tail_xs, p_rows + SUB, slab_rows)
        ys = _expert(xs, w_gate[l], b_gate[l], w_up[l], b_up[l], w_down[l], b_down[l],
                     vexp, vrow, vnsub, tail_ys, p_rows)
        xf = _combine(ys, dest_flat, wgt, x1, mod3, seq, g_post_ffn[l].reshape(1, d))
    return xf.reshape(bsz, seq, d)
```

```python
import functools

import jax
import jax.numpy as jnp
from jax import lax
from jax.experimental import pallas as pl
from jax.experimental.pallas import tpu as pltpu

EPS = 1e-6
TOP_K = 4
CONV_A = 3
CONV_B = 31
SWIGLU_ALPHA = 1.702
SWIGLU_LIMIT = 7.0

LANES = 128
SUBLANES = 8
VMEM_LIMIT_BYTES = 63 * 1024 * 1024

ADA_TN = 1024
MIX_TM = 256
HALO_A = 8
HALO_B = 32
PLAN_TM = 2048
DISPATCH_TM = 1024
DISPATCH_UNROLL = 8
SUB = 128
EXPERT_ROWS = 2176
EXPERT_TF = 512
CHAIN_MAX = 8
CHAIN_PIECE = 8
X_AHEAD = 7
X_SLOTS = X_AHEAD + 1
O_SLOTS = 5
COMBINE_TM = 256
COMBINE_UNROLL = 8
NEG_BIG = -1e30


def _cparams(sem):
    return pltpu.CompilerParams(dimension_semantics=sem, vmem_limit_bytes=VMEM_LIMIT_BYTES)


def _rms(x, g):
    return x * lax.rsqrt(jnp.mean(x * x, axis=-1, keepdims=True) + EPS) * g


def _pack_slab_words(v):
    half = v.shape[1] // 2
    lo = pltpu.bitcast(v[:, :half].astype(jnp.bfloat16).astype(jnp.float32), jnp.uint32)
    hi = pltpu.bitcast(v[:, half:].astype(jnp.bfloat16).astype(jnp.float32), jnp.uint32)
    return (lo >> 16) | (hi & jnp.uint32(0xFFFF0000))


def _unpack_slab_words(w):
    lo = pltpu.bitcast(w << 16, jnp.float32)
    hi = pltpu.bitcast(w & jnp.uint32(0xFFFF0000), jnp.float32)
    return lo, hi


def _ada_kernel(c_ref, w_ref, b_ref, o_ref):
    c = c_ref[...]
    ca = (c * jax.nn.sigmoid(c)).astype(jnp.bfloat16)
    o_ref[...] = jnp.dot(ca, w_ref[...].astype(jnp.bfloat16),
                         preferred_element_type=jnp.float32) + b_ref[...]


def _ada(c, w_ada, b_ada):
    bsz, d = c.shape
    n_out = w_ada.shape[1]
    tn = min(ADA_TN, n_out)
    assert n_out % tn == 0 and bsz <= SUBLANES
    c8 = jnp.zeros((SUBLANES, d), jnp.float32).at[:bsz].set(c)
    out = pl.pallas_call(
        _ada_kernel,
        grid=(n_out // tn,),
        in_specs=[pl.BlockSpec((SUBLANES, d), lambda j: (0, 0)),
                  pl.BlockSpec((d, tn), lambda j: (0, j)),
                  pl.BlockSpec((1, tn), lambda j: (0, j))],
        out_specs=pl.BlockSpec((SUBLANES, tn), lambda j: (0, j)),
        out_shape=jax.ShapeDtypeStruct((SUBLANES, n_out), jnp.float32),
        compiler_params=_cparams(("arbitrary",)),
        name="ada",
    )(c8, w_ada, b_ada.reshape(1, n_out))
    return out[:bsz]


def _mix_kernel(x_ref, xp_ref, mod_ref, modp_ref, gpre_ref, gpost_ref, gffn_ref, win_ref,
                caw_ref, cbw_ref, cbb_ref, lng_ref, lnb_ref, wout_ref, wrc_ref, wrh_ref, br_ref,
                x1_ref, hp_ref, idx_ref, wgt_ref, rank_ref, cnt_ref,
                pa_buf, u_buf, v_buf, ycat, run_cnt, *, tiles_per_seq, wa, wb):
    i = pl.program_id(0)
    tm, d = x_ref.shape
    f32, bf16 = jnp.float32, jnp.bfloat16

    @pl.when(i == 0)
    def _():
        run_cnt[...] = jnp.zeros_like(run_cnt)
        ycat[...] = jnp.zeros_like(ycat)

    @pl.when(i % tiles_per_seq == 0)
    def _():
        pa_buf[:, 0:HALO_A, :] = jnp.zeros((wa // LANES, HALO_A, LANES), f32)
        u_buf[:, 0:HALO_B, :] = jnp.zeros((wb // LANES, HALO_B, LANES), f32)

    y_prev = jnp.dot(ycat[...], wout_ref[...], preferred_element_type=f32)

    x = x_ref[...]
    mod = mod_ref[0]
    h = (_rms(x, gpre_ref[...]) * (1.0 + mod[1:2]) + mod[0:1]).astype(bf16)
    proj_b = jnp.dot(h, win_ref[:, 3 * wa:3 * wa + 2 * wb], preferred_element_type=f32)
    proj_a = jnp.dot(h, win_ref[:, 0:3 * wa], preferred_element_type=f32)

    for cb in range(wa // LANES):
        sl = slice(cb * LANES, (cb + 1) * LANES)
        pa = proj_a[:, sl] * proj_a[:, 2 * wa + cb * LANES:2 * wa + (cb + 1) * LANES]
        pa_buf[cb, HALO_A:HALO_A + tm, :] = pa
        conv = caw_ref[CONV_A - 1:CONV_A, sl] * pa
        for k in range(CONV_A - 1):
            off = HALO_A - (CONV_A - 1) + k
            conv = conv + caw_ref[k:k + 1, sl] * pa_buf[cb, off:off + tm, :]
        ycat[:, sl] = (proj_a[:, wa + cb * LANES:wa + (cb + 1) * LANES] * conv).astype(bf16)
        pa_buf[cb, 0:HALO_A, :] = pa_buf[cb, tm:tm + HALO_A, :]

    for cb in range(wb // LANES):
        sl = slice(cb * LANES, (cb + 1) * LANES)
        u = proj_b[:, sl] * jax.nn.sigmoid(proj_b[:, wb + cb * LANES:wb + (cb + 1) * LANES])
        u_buf[cb, HALO_B:HALO_B + tm, :] = u
        acc = cbb_ref[:, sl] + cbw_ref[CONV_B - 1:CONV_B, sl] * u
        for k in range(CONV_B - 1):
            off = HALO_B - (CONV_B - 1) + k
            acc = acc + cbw_ref[k:k + 1, sl] * u_buf[cb, off:off + tm, :]
        v_buf[:, sl] = acc
        u_buf[cb, 0:HALO_B, :] = u_buf[cb, tm:tm + HALO_B, :]
    v = v_buf[...]
    mu = jnp.mean(v, axis=-1, keepdims=True)
    vc = v - mu
    var = jnp.mean(vc * vc, axis=-1, keepdims=True)
    yb = vc * lax.rsqrt(var + EPS) * lng_ref[...] + lnb_ref[...]
    ycat[:, wa:wa + wb] = (yb * jax.nn.sigmoid(yb)).astype(bf16)

    _mix_stage2(i >= 1, y_prev, xp_ref, modp_ref, gpost_ref, gffn_ref, wrc_ref, wrh_ref, br_ref,
                x1_ref, hp_ref, idx_ref, wgt_ref, rank_ref, cnt_ref, run_cnt)


def _mix_stage2(valid, y, x_ref, mod_ref, gpost_ref, gffn_ref, wrc_ref, wrh_ref, br_ref,
                x1_ref, hp_ref, idx_ref, wgt_ref, rank_ref, cnt_ref, run_cnt):
    tm, d = x_ref.shape
    f32, bf16 = jnp.float32, jnp.bfloat16
    x = x_ref[...]
    mod = mod_ref[0]
    x1 = x + mod[2:3] * _rms(y, gpost_ref[...])
    x1_ref[...] = x1

    h2 = _rms(x1, gffn_ref[...]) * (1.0 + mod[4:5]) + mod[3:4]
    words = _pack_slab_words(h2)
    slab_rows = d // (2 * LANES)
    for j in range(slab_rows):
        hp_ref[pl.ds(j, tm, stride=slab_rows), :] = words[:, j * LANES:(j + 1) * LANES]

    h2_hi = h2.astype(bf16)
    h2_lo = (h2 - h2_hi.astype(f32)).astype(bf16)
    both = jnp.dot(h2_hi, wrc_ref[...], preferred_element_type=f32)
    logits = (both[:, :LANES] + both[:, LANES:]
              + jnp.dot(h2_lo, wrh_ref[...], preferred_element_type=f32)
              + br_ref[...])

    lane = lax.broadcasted_iota(jnp.int32, (tm, LANES), 1)
    lane_f = lane.astype(f32)
    vals, idxs = [], []
    cur = logits
    for _ in range(TOP_K):
        m = jnp.max(cur, axis=-1, keepdims=True)
        ix = jnp.min(jnp.where(cur == m, lane_f, float(LANES)), axis=-1,
                     keepdims=True).astype(jnp.int32)
        vals.append(m)
        idxs.append(ix)
        cur = jnp.where(lane == ix, -jnp.inf, cur)
    exps = [jnp.exp(vk - vals[0]) for vk in vals]
    denom = exps[0]
    for ek in exps[1:]:
        denom = denom + ek
    inv = 1.0 / denom

    row = lax.broadcasted_iota(jnp.int32, (tm, tm), 0)
    col = lax.broadcasted_iota(jnp.int32, (tm, tm), 1)
    tri = jnp.where(col < row, 1.0, 0.0).astype(bf16)
    run = run_cnt[...]
    idx_out = jnp.zeros((tm, LANES), jnp.int32)
    wgt_out = jnp.zeros((tm, LANES), f32)
    rank_out = jnp.zeros((tm, LANES), f32)
    onehots = [jnp.where(lane == idxs[k], 1.0, 0.0) for k in range(TOP_K)]
    before_all = jnp.dot(tri, jnp.concatenate(onehots, axis=1).astype(bf16),
                         preferred_element_type=f32)
    for k in range(TOP_K):
        oh = onehots[k]
        before = before_all[:, k * LANES:(k + 1) * LANES]
        rank_k = jnp.sum(oh * (before + run), axis=-1, keepdims=True)
        run = run + jnp.sum(oh, axis=0, keepdims=True)
        idx_out = jnp.where(lane == k, idxs[k], idx_out)
        wgt_out = jnp.where(lane == k, exps[k] * inv, wgt_out)
        rank_out = jnp.where(lane == k, rank_k, rank_out)
    run = jnp.where(valid, run, run_cnt[...])
    run_cnt[...] = run
    idx_ref[...] = idx_out
    wgt_ref[...] = wgt_out
    rank_ref[...] = rank_out.astype(jnp.int32)
    cnt_ref[...] = jnp.broadcast_to(run, cnt_ref.shape).astype(jnp.int32)


def _mix(x2, mod3, seq, g_pre, g_post, g_ffn, w_in, conv_a_w, conv_b_w, conv_b_b, ln_g, ln_b,
         w_out, wr_cat, wr_hi, br):
    n, d = x2.shape
    wa = conv_a_w.shape[1]
    wb = conv_b_w.shape[1]
    tm = MIX_TM
    assert seq % tm == 0 and tm >= HALO_B and wa % LANES == 0 and wb % LANES == 0
    assert wa + wb == d and d % (2 * LANES) == 0
    tiles_per_seq = seq // tm
    n_tiles = n // tm
    slab_rows = d // (2 * LANES)
    cur = lambda i: jnp.minimum(i, n_tiles - 1)
    prev = lambda i: jnp.maximum(i - 1, 0)
    const = lambda shape: pl.BlockSpec(shape, lambda i: (0,) * len(shape))
    resident = lambda shape: pl.BlockSpec(shape, lambda i: (0,) * len(shape),
                                          pipeline_mode=pl.Buffered(1))
    out_block = lambda w: pl.BlockSpec((tm, w), lambda i: (prev(i), 0))
    mod_block = lambda tile: pl.BlockSpec((1, mod3.shape[1], d),
                                          lambda i: (tile(i) // tiles_per_seq, 0, 0))
    kernel = functools.partial(_mix_kernel, tiles_per_seq=tiles_per_seq, wa=wa, wb=wb)
    return pl.pallas_call(
        kernel,
        grid=(n_tiles + 1,),
        in_specs=[pl.BlockSpec((tm, d), lambda i: (cur(i), 0)),
                  pl.BlockSpec((tm, d), lambda i: (prev(i), 0)),
                  mod_block(cur), mod_block(prev),
                  const((1, d)), const((1, d)), const((1, d)),
                  resident(w_in.shape),
                  const(conv_a_w.shape), const(conv_b_w.shape),
                  const((1, wb)), const((1, wb)), const((1, wb)),
                  resident(w_out.shape),
                  const(wr_cat.shape), const(wr_hi.shape), const((1, LANES))],
        out_specs=[out_block(d),
                   pl.BlockSpec((tm * slab_rows, LANES), lambda i: (prev(i), 0)),
                   out_block(LANES), out_block(LANES), out_block(LANES),
                   const((SUBLANES, LANES))],
        out_shape=[jax.ShapeDtypeStruct((n, d), jnp.float32),
                   jax.ShapeDtypeStruct((n * slab_rows, LANES), jnp.uint32),
                   jax.ShapeDtypeStruct((n, LANES), jnp.int32),
                   jax.ShapeDtypeStruct((n, LANES), jnp.float32),
                   jax.ShapeDtypeStruct((n, LANES), jnp.int32),
                   jax.ShapeDtypeStruct((SUBLANES, LANES), jnp.int32)],
        scratch_shapes=[pltpu.VMEM((wa // LANES, tm + HALO_A, LANES), jnp.float32),
                        pltpu.VMEM((wb // LANES, tm + HALO_B, LANES), jnp.float32),
                        pltpu.VMEM((tm, wb), jnp.float32),
                        pltpu.VMEM((tm, d), jnp.bfloat16),
                        pltpu.VMEM((1, LANES), jnp.float32)],
        compiler_params=_cparams(("arbitrary",)),
        name="mix",
    )(x2, x2, mod3, mod3, g_pre, g_post, g_ffn, w_in, conv_a_w, conv_b_w, conv_b_b, ln_g, ln_b,
      w_out, wr_cat, wr_hi, br)


def _plan_kernel(idx_ref, rank_ref, start_ref, dest_ref):
    tm = idx_ref.shape[0]
    start = jnp.broadcast_to(start_ref[...], (tm, LANES))
    dest_ref[...] = jnp.take_along_axis(start, idx_ref[...], axis=1) + rank_ref[...]


def _plan(idx, rank, start_row):
    n = idx.shape[0]
    tm = min(PLAN_TM, n)
    assert n % tm == 0
    blk = pl.BlockSpec((tm, LANES), lambda i: (i, 0))
    return pl.pallas_call(
        _plan_kernel,
        grid=(n // tm,),
        in_specs=[blk, blk, pl.BlockSpec((1, LANES), lambda i: (0, 0))],
        out_specs=blk,
        out_shape=jax.ShapeDtypeStruct((n, LANES), jnp.int32),
        compiler_params=_cparams(("arbitrary",)),
        name="plan",
    )(idx, rank, start_row)


def _slab(ref, row, slab_rows, count=1):
    return ref.at[pl.ds(pl.multiple_of(row * slab_rows, slab_rows), count * slab_rows)]


def _dispatch_kernel(zstart_ref, tail_ref, dest_ref, hp_ref, xs_ref, zbuf, sem, zsem,
                     *, tm, n_experts, slab_rows):
    i = pl.program_id(0)

    def zero_copy(start):
        return pltpu.make_async_copy(zbuf, _slab(xs_ref, start, slab_rows, SUB), zsem)

    @pl.when(i == 0)
    def _():
        zbuf[...] = jnp.zeros_like(zbuf)
        for parity in range(2):
            for e in range(parity, n_experts, 2):
                zero_copy(zstart_ref[e]).start()
            for e in range(parity, n_experts, 2):
                zero_copy(zstart_ref[e]).wait()

        def tail_start(t, carry):
            zero_copy(tail_ref[0] + t * SUB).start()
            return carry

        def tail_wait(t, carry):
            zero_copy(tail_ref[0] + t * SUB).wait()
            return carry
        lax.fori_loop(0, tail_ref[1], tail_start, 0)
        lax.fori_loop(0, tail_ref[1], tail_wait, 0)

    def body(t, carry):
        for u in range(DISPATCH_UNROLL):
            tok = t * DISPATCH_UNROLL + u
            src = _slab(hp_ref, tok, slab_rows)
            for k in range(TOP_K):
                dst = _slab(xs_ref, dest_ref[0, 0, tok * TOP_K + k], slab_rows)
                pltpu.make_async_copy(src, dst, sem).start(priority=k % 2)
        return carry
    lax.fori_loop(0, tm // DISPATCH_UNROLL, body, 0)
    for k in range(TOP_K):
        pltpu.make_async_copy(hp_ref, _slab(xs_ref, 0, slab_rows, tm), sem).wait()


def _dispatch(hp, dest_flat, zstart, tail, p_rows, slab_rows):
    n = hp.shape[0] // slab_rows
    tm = min(DISPATCH_TM, n)
    assert n % tm == 0 and tm % DISPATCH_UNROLL == 0
    n_experts = zstart.shape[0]
    dest3 = dest_flat.reshape(n // tm, 1, tm * TOP_K)
    kernel = functools.partial(_dispatch_kernel, tm=tm, n_experts=n_experts, slab_rows=slab_rows)
    return pl.pallas_call(
        kernel,
        grid_spec=pltpu.PrefetchScalarGridSpec(
            num_scalar_prefetch=2,
            grid=(n // tm,),
            in_specs=[pl.BlockSpec((1, 1, tm * TOP_K), lambda i, *_: (i, 0, 0),
                                   memory_space=pltpu.SMEM),
                      pl.BlockSpec((tm * slab_rows, LANES), lambda i, *_: (i, 0))],
            out_specs=pl.BlockSpec(memory_space=pl.ANY),
            scratch_shapes=[pltpu.VMEM((SUB * slab_rows, LANES), jnp.uint32),
                            pltpu.SemaphoreType.DMA(()),
                            pltpu.SemaphoreType.DMA(())]),
        out_shape=jax.ShapeDtypeStruct((p_rows * slab_rows, LANES), jnp.uint32),
        compiler_params=_cparams(("arbitrary",)),
        name="dispatch",
    )(zstart, tail, dest3, hp)


def _expert_kernel(vexp_ref, vrow_ref, vnsub_ref, tail_ref,
                   xs_ref, wg_ref, wu_ref, wd_ref, bg_ref, bu_ref, bd_ref, ys_ref,
                   xb, acc, xstage, ostage, xsem, osem, zsem, *, n_f):
    del vexp_ref
    v = pl.program_id(0)
    j = pl.program_id(1)
    d = xb.shape[1]
    slab_rows = d // (2 * LANES)
    stage_rows = SUB * slab_rows
    f32, bf16 = jnp.float32, jnp.bfloat16
    nsub = vnsub_ref[v]
    row0 = vrow_ref[v]

    def stage_slot(ref, slot):
        return ref.at[pl.ds(pl.multiple_of(slot * stage_rows, stage_rows), stage_rows)]

    def x_copy(first_row, s):
        slot = s % X_SLOTS
        return pltpu.make_async_copy(_slab(xs_ref, first_row + s * SUB, slab_rows, SUB),
                                     stage_slot(xstage, slot), xsem.at[slot])

    def x_prefetch(first_row, count):
        for s in range(X_AHEAD):
            @pl.when(s < count)
            def _():
                x_copy(first_row, s).start()

    def o_copy(s, slot):
        return pltpu.make_async_copy(stage_slot(ostage, slot),
                                     _slab(ys_ref, row0 + s * SUB, slab_rows, SUB), osem.at[slot])

    @pl.when((v == 0) & (j == 0))
    def _():
        ostage[0:stage_rows, :] = jnp.zeros((stage_rows, LANES), jnp.uint32)

        def z_copy(t):
            return pltpu.make_async_copy(stage_slot(ostage, 0),
                                         _slab(ys_ref, tail_ref[0] + t * SUB, slab_rows, SUB), zsem)

        def z_start(t, carry):
            z_copy(t).start()
            return carry

        def z_wait(t, carry):
            z_copy(t).wait()
            return carry
        lax.fori_loop(0, tail_ref[1], z_start, 0)
        lax.fori_loop(0, tail_ref[1], z_wait, 0)
        x_prefetch(row0, nsub)

    @pl.when(j == 0)
    def _():
        def body(s, carry):
            slot = s % X_SLOTS

            @pl.when(s + X_AHEAD < nsub)
            def _():
                x_copy(row0, s + X_AHEAD).start()
            x_copy(row0, s).wait()
            base = pl.multiple_of(s * SUB, SUB)
            for jj in range(slab_rows):
                lo, hi = _unpack_slab_words(
                    xstage[pl.ds(slot * stage_rows + jj, SUB, stride=slab_rows), :])
                xb[pl.ds(base, SUB), jj * LANES:(jj + 1) * LANES] = lo.astype(bf16)
                xb[pl.ds(base, SUB), d // 2 + jj * LANES:d // 2 + (jj + 1) * LANES] = hi.astype(bf16)
            acc[pl.ds(base, SUB), :] = jnp.zeros((SUB, d), f32)
            return carry
        lax.fori_loop(0, nsub, body, 0)

    def chain(base, m):
        wg = wg_ref[...].astype(bf16)
        wu = wu_ref[...].astype(bf16)
        wd = wd_ref[...].astype(bf16)
        piece = min(m, CHAIN_PIECE * SUB)
        for p in range(m // piece):
            rows = pl.ds(pl.multiple_of(base + p * piece, piece), piece)
            xt = xb[rows, :]
            g = jnp.dot(xt, wg, preferred_element_type=f32) + bg_ref[...]
            u = jnp.dot(xt, wu, preferred_element_type=f32) + bu_ref[...]
            g = jnp.minimum(g, SWIGLU_LIMIT)
            u = jnp.clip(u, -SWIGLU_LIMIT, SWIGLU_LIMIT)
            a = (u + 1.0) * (g * jax.nn.sigmoid(SWIGLU_ALPHA * g))
            acc[rows, :] += jnp.dot(a.astype(bf16), wd, preferred_element_type=f32)

    def big_body(q, carry):
        chain(pl.multiple_of(q * (CHAIN_MAX * SUB), CHAIN_MAX * SUB), CHAIN_MAX * SUB)
        return carry
    lax.fori_loop(0, nsub // CHAIN_MAX, big_body, 0)
    units = CHAIN_MAX // 2
    while units >= 1:
        @pl.when((nsub & units) != 0)
        def _(units=units):
            start = (nsub // (2 * units)) * (2 * units)
            chain(pl.multiple_of(start * SUB, units * SUB), units * SUB)
        units //= 2

    @pl.when((j == n_f - 1) & (v + 1 < pl.num_programs(0)))
    def _():
        nxt = jnp.minimum(v + 1, pl.num_programs(0) - 1)
        x_prefetch(vrow_ref[nxt], vnsub_ref[nxt])

    @pl.when((j == n_f - 1) & (nsub > 0))
    def _():
        def body(s, carry):
            slot = s % O_SLOTS

            @pl.when(s >= O_SLOTS)
            def _():
                o_copy(s - O_SLOTS, slot).wait()
            words = _pack_slab_words(acc[pl.ds(pl.multiple_of(s * SUB, SUB), SUB), :] + bd_ref[...])
            for jj in range(slab_rows):
                ostage[pl.ds(slot * stage_rows + jj, SUB, stride=slab_rows), :] = (
                    words[:, jj * LANES:(jj + 1) * LANES])
            o_copy(s, slot).start()
            return carry
        lax.fori_loop(0, nsub, body, 0)

        def drain(s, carry):
            o_copy(s, s % O_SLOTS).wait()
            return carry
        lax.fori_loop(jnp.maximum(nsub - O_SLOTS, 0), nsub, drain, 0)


def _expert(xs, w_gate, b_gate, w_up, b_up, w_down, b_down, vexp, vrow, vnsub, tail, p_rows):
    n_experts, d, f = w_gate.shape
    tf = min(EXPERT_TF, f)
    assert f % tf == 0 and EXPERT_ROWS % SUB == 0
    n_f = f // tf
    slab_rows = d // (2 * LANES)
    n_visits = vexp.shape[0]

    def w_idx(v, j, ve, vr, vn, tl):
        return jnp.where(vn[v] == 0, n_f - 1, j)

    kernel = functools.partial(_expert_kernel, n_f=n_f)
    return pl.pallas_call(
        kernel,
        grid_spec=pltpu.PrefetchScalarGridSpec(
            num_scalar_prefetch=4,
            grid=(n_visits, n_f),
            in_specs=[
                pl.BlockSpec(memory_space=pl.ANY),
                pl.BlockSpec((None, d, tf), lambda v, j, ve, *s: (ve[v], 0, w_idx(v, j, ve, *s))),
                pl.BlockSpec((None, d, tf), lambda v, j, ve, *s: (ve[v], 0, w_idx(v, j, ve, *s))),
                pl.BlockSpec((None, tf, d), lambda v, j, ve, *s: (ve[v], w_idx(v, j, ve, *s), 0)),
                pl.BlockSpec((None, 1, tf), lambda v, j, ve, *s: (ve[v], 0, w_idx(v, j, ve, *s))),
                pl.BlockSpec((None, 1, tf), lambda v, j, ve, *s: (ve[v], 0, w_idx(v, j, ve, *s))),
                pl.BlockSpec((None, 1, d), lambda v, j, ve, *s: (ve[v], 0, 0)),
            ],
            out_specs=pl.BlockSpec(memory_space=pl.ANY),
            scratch_shapes=[pltpu.VMEM((EXPERT_ROWS, d), jnp.bfloat16),
                            pltpu.VMEM((EXPERT_ROWS, d), jnp.float32),
                            pltpu.VMEM((X_SLOTS * SUB * slab_rows, LANES), jnp.uint32),
                            pltpu.VMEM((O_SLOTS * SUB * slab_rows, LANES), jnp.uint32),
                            pltpu.SemaphoreType.DMA((X_SLOTS,)),
                            pltpu.SemaphoreType.DMA((O_SLOTS,)),
                            pltpu.SemaphoreType.DMA(())]),
        out_shape=jax.ShapeDtypeStruct((p_rows * slab_rows, LANES), jnp.uint32),
        compiler_params=_cparams(("arbitrary", "arbitrary")),
        name="expert",
    )(vexp, vrow, vnsub, tail, xs, w_gate, w_up, w_down,
      b_gate.reshape(n_experts, 1, f), b_up.reshape(n_experts, 1, f),
      b_down.reshape(n_experts, 1, d))


def _combine_kernel(dest_ref, dnext_ref, ys_ref, wgt_ref, x1_ref, mod_ref, g_ref, o_ref,
                    buf, ybuf, sem, *, tm):
    i = pl.program_id(0)
    d = x1_ref.shape[1]
    slab_rows = d // (2 * LANES)
    half = TOP_K * tm

    def gather(idx_ref, slot):
        def body(t, carry):
            for u in range(COMBINE_UNROLL):
                tok = t * COMBINE_UNROLL + u
                for k in range(TOP_K):
                    src = _slab(ys_ref, idx_ref[0, 0, tok * TOP_K + k], slab_rows)
                    dst = _slab(buf, slot * half + k * tm + tok, slab_rows)
                    pltpu.make_async_copy(src, dst, sem.at[slot]).start(priority=k % 2)
            return carry
        lax.fori_loop(0, tm // COMBINE_UNROLL, body, 0)

    slot = i % 2

    @pl.when(i == 0)
    def _():
        gather(dest_ref, 0)

    @pl.when(i + 1 < pl.num_programs(0))
    def _():
        gather(dnext_ref, 1 - slot)

    pltpu.make_async_copy(_slab(ys_ref, 0, slab_rows, half), _slab(buf, slot * half, slab_rows, half),
                          sem.at[slot]).wait()

    wgt = wgt_ref[...]
    base = slot * half * slab_rows
    for jj in range(slab_rows):
        ylo = jnp.zeros((tm, LANES), jnp.float32)
        yhi = jnp.zeros((tm, LANES), jnp.float32)
        for k in range(TOP_K):
            lo, hi = _unpack_slab_words(
                buf[pl.ds(base + k * tm * slab_rows + jj, tm, stride=slab_rows), :])
            ylo = ylo + wgt[:, k:k + 1] * lo
            yhi = yhi + wgt[:, k:k + 1] * hi
        ybuf[:, jj * LANES:(jj + 1) * LANES] = ylo
        ybuf[:, d // 2 + jj * LANES:d // 2 + (jj + 1) * LANES] = yhi
    mod = mod_ref[0]
    o_ref[...] = x1_ref[...] + mod[5:6] * _rms(ybuf[...], g_ref[...])


def _combine(ys, dest_flat, wgt, x1, mod3, seq, g_post):
    n, d = x1.shape
    slab_rows = d // (2 * LANES)
    tm = COMBINE_TM
    assert seq % tm == 0
    tiles_per_seq = seq // tm
    n_tiles = n // tm
    dest3 = dest_flat.reshape(n_tiles, 1, tm * TOP_K)
    kernel = functools.partial(_combine_kernel, tm=tm)
    return pl.pallas_call(
        kernel,
        grid=(n_tiles,),
        in_specs=[pl.BlockSpec((1, 1, tm * TOP_K), lambda i: (i, 0, 0), memory_space=pltpu.SMEM),
                  pl.BlockSpec((1, 1, tm * TOP_K), lambda i: (jnp.minimum(i + 1, n_tiles - 1), 0, 0),
                               memory_space=pltpu.SMEM),
                  pl.BlockSpec(memory_space=pl.ANY),
                  pl.BlockSpec((tm, LANES), lambda i: (i, 0)),
                  pl.BlockSpec((tm, d), lambda i: (i, 0)),
                  pl.BlockSpec((1, mod3.shape[1], d), lambda i: (i // tiles_per_seq, 0, 0)),
                  pl.BlockSpec((1, d), lambda i: (0, 0))],
        out_specs=pl.BlockSpec((tm, d), lambda i: (i, 0)),
        out_shape=jax.ShapeDtypeStruct((n, d), jnp.float32),
        scratch_shapes=[pltpu.VMEM((2 * TOP_K * tm * slab_rows, LANES), jnp.uint32),
                        pltpu.VMEM((tm, d), jnp.float32),
                        pltpu.SemaphoreType.DMA((2,))],
        compiler_params=_cparams(("arbitrary",)),
        name="combine",
    )(dest3, dest3, ys, wgt, x1, mod3, g_post)


def _routing_tables(counts, n_assign):
    n_experts = counts.shape[0]
    p_rows = n_assign + n_experts * SUB
    n_visits = n_experts + -(-p_rows // EXPERT_ROWS)

    padded = jnp.maximum((counts + SUB - 1) // SUB, 1) * SUB
    e_ids = jnp.arange(n_experts, dtype=jnp.int32)
    lower = e_ids[None, :] <= e_ids[:, None]
    pad_end = jnp.sum(jnp.where(lower, padded[None, :], 0), axis=1)
    pad_start = pad_end - padded
    total = pad_end[-1]
    zstart = (pad_start + counts).astype(jnp.int32)
    tail_xs = jnp.stack([total, (p_rows + SUB - total) // SUB]).astype(jnp.int32)
    tail_ys = jnp.stack([total, (p_rows - total) // SUB]).astype(jnp.int32)

    n_chunk = (padded + EXPERT_ROWS - 1) // EXPERT_ROWS
    chunk_end = jnp.sum(jnp.where(lower, n_chunk[None, :], 0), axis=1)
    visit = jnp.arange(n_visits, dtype=jnp.int32)
    used = visit < chunk_end[-1]
    vexp = jnp.minimum(jnp.sum((chunk_end[None, :] <= visit[:, None]).astype(jnp.int32), axis=1),
                       n_experts - 1)
    onehot = vexp[:, None] == e_ids[None, :]
    pick = lambda a: jnp.sum(jnp.where(onehot, a[None, :], 0), axis=1)
    chunk = visit - pick(chunk_end - n_chunk)
    vrow = jnp.where(used, pick(pad_start) + chunk * EXPERT_ROWS, 0)
    vnsub = jnp.where(used, jnp.minimum(pick(padded) - chunk * EXPERT_ROWS, EXPERT_ROWS) // SUB, 0)
    return (pad_start.astype(jnp.int32), zstart, tail_xs, tail_ys, vexp.astype(jnp.int32),
            vrow.astype(jnp.int32), vnsub.astype(jnp.int32), p_rows)


def kernel(x, c, w_ada, b_ada, g_pre_mix, g_post_mix, w_in, conv_a_w, conv_b_w, conv_b_b, ln_b_g, ln_b_b, w_out, g_pre_ffn, g_post_ffn, w_router, b_router, w_gate, b_gate, w_up, b_up, w_down, b_down):
    bsz, seq, d = x.shape
    n = bsz * seq
    depth = w_ada.shape[0]
    n_experts = w_router.shape[-1]
    assert n_experts <= LANES
    xf = x.reshape(n, d)
    for l in range(depth):
        mod = _ada(c, w_ada[l], b_ada[l])
        n_mod = mod.shape[1] // d
        mod3 = mod.reshape(bsz, n_mod, d)

        wr = jnp.zeros((d, LANES), jnp.float32).at[:, :n_experts].set(w_router[l])
        wr_hi = wr.astype(jnp.bfloat16)
        wr_lo = (wr - wr_hi.astype(jnp.float32)).astype(jnp.bfloat16)
        br = jnp.full((1, LANES), NEG_BIG, jnp.float32).at[0, :n_experts].set(b_router[l])

        x1, hp, idx, wgt, rank, cnt = _mix(
            xf, mod3, seq, g_pre_mix[l].reshape(1, d), g_post_mix[l].reshape(1, d),
            g_pre_ffn[l].reshape(1, d), w_in[l].astype(jnp.bfloat16), conv_a_w[l], conv_b_w[l],
            conv_b_b[l].reshape(1, -1), ln_b_g[l].reshape(1, -1), ln_b_b[l].reshape(1, -1),
            w_out[l].astype(jnp.bfloat16), jnp.concatenate([wr_hi, wr_lo], axis=1), wr_hi, br)

        (pad_start, zstart, tail_xs, tail_ys, vexp, vrow, vnsub,
         p_rows) = _routing_tables(cnt[0, :n_experts], n * TOP_K)
        start_row = jnp.zeros((1, LANES), jnp.int32).at[0, :n_experts].set(pad_start)
        dest = _plan(idx, rank, start_row)
        dest_flat = dest[:, :TOP_K].reshape(n * TOP_K)

        slab_rows = d // (2 * LANES)
        xs = _dispatch(hp, dest_flat, zstart, tail_xs, p_rows + SUB, slab_rows)
        ys = _expert(xs, w_gate[l], b_gate[l], w_up[l], b_up[l], w_down[l], b_down[l],
                     vexp, vrow, vnsub, tail_ys, p_rows)
        xf = _combine(ys, dest_flat, wgt, x1, mod3, seq, g_post_ffn[l].reshape(1, d))
    return xf.reshape(bsz, seq, d)
```
